```python
import functools
import jax, jax.numpy as jnp
from jax import lax
import numpy as np

D_MODEL = 1024
BATCH = 16
SEQ = 2048
DEPTH = 2
DEC_BATCH = 32
DEC_SEQ = 8
PAST_LEN = 16384
PAGE_SIZE = 128

PLE_DIM = 256
N_A_LAYERS = DEPTH // 2
N_B_LAYERS = DEPTH - N_A_LAYERS
N_DENSE = (DEPTH + 1) // 2
N_MOE = DEPTH // 2
DEEPNORM_ALPHA = (2.0 * DEPTH) ** 0.25
DEEPNORM_BETA = (8.0 * DEPTH) ** -0.25
LN_EPS = 1e-5
ROPE_THETA = 10000.0

GLA_HEADS = 4
GLA_DK = D_MODEL // 2 // GLA_HEADS
GLA_DV = D_MODEL // GLA_HEADS
GLA_GATE_RANK = 16
GLA_TAU = 16.0
GLA_CHUNK = 64
GLA_IN = 2 * GLA_HEADS * GLA_DK + 2 * GLA_HEADS * GLA_DV + GLA_GATE_RANK

NSA_HEADS = 16
NSA_HD = D_MODEL // NSA_HEADS
NSA_GROUPS = 2
NSA_HPG = NSA_HEADS // NSA_GROUPS
CMP_STRIDE = 16
CMP_LEN = 2 * CMP_STRIDE
CMP_HIDDEN = 2 * NSA_HD
SEL_BLOCK = 64
SEL_TOP = 16
WINDOW = 512
SEL_Q_BLOCK = 32
WIN_Q_BLOCK = 128
NSA_IN = NSA_HEADS * NSA_HD + 3 * NSA_HEADS
NSA_KV = 6 * NSA_GROUPS * NSA_HD
NEG_INF = -1e30
FORCE_SCORE = 1e9

D_FF_DENSE = 2816
N_EXPERTS = 8
TOP_K = 2
D_FF_EXPERT = 3584

kernel_name = 'yoco_gla_nsa_moe_decoder_step'


def _pad_axis(x, axis, size):
    pad = [(0, 0)] * x.ndim
    pad[axis] = (0, size - x.shape[axis])
    return jnp.pad(x, pad)


def _pad_front(x, n):
    pad = [(0, 0)] * x.ndim
    pad[1] = (n, 0)
    return jnp.pad(x, pad)


def _layer_norm(x, g, b):
    xf = x.astype(jnp.float32)
    mu = xf.mean(-1, keepdims=True)
    var = jnp.square(xf - mu).mean(-1, keepdims=True)
    return ((xf - mu) * lax.rsqrt(var + LN_EPS) * g + b).astype(x.dtype)


def _rope(x, pos):
    half = x.shape[-1] // 2
    inv = ROPE_THETA ** (-jnp.arange(half, dtype=jnp.float32) / half)
    ang = pos.astype(jnp.float32)[:, None] * inv[None, :]
    cos, sin = jnp.cos(ang)[:, None, :], jnp.sin(ang)[:, None, :]
    xf = x.astype(jnp.float32)
    x1, x2 = xf[..., :half], xf[..., half:]
    return jnp.concatenate([x1 * cos - x2 * sin, x2 * cos + x1 * sin], axis=-1).astype(x.dtype)


def _masked_softmax(s, mask):
    s = jnp.where(mask, s.astype(jnp.float32), NEG_INF)
    e = jnp.where(mask, jnp.exp(s - s.max(-1, keepdims=True)), 0.0)
    return e / jnp.maximum(e.sum(-1, keepdims=True), 1e-30)


def _blocked_map(fn, qb, *xs):
    B, T = xs[0].shape[:2]
    nb = -(-T // qb)

    def to_blocks(x):
        x = _pad_axis(x, 1, nb * qb)
        return jnp.moveaxis(x.reshape(B, nb, qb, *x.shape[2:]), 1, 0)

    out = lax.map(lambda a: fn(a[0], *a[1:]), (jnp.arange(nb, dtype=jnp.int32), *[to_blocks(x) for x in xs]))
    out = jnp.moveaxis(out, 0, 1)
    return out.reshape(B, nb * qb, *out.shape[3:])[:, :T]


def _swiglu(x, w_gate, w_up, w_down):
    return (jax.nn.silu(x @ w_gate) * (x @ w_up)) @ w_down


def _moe(x, w_router, w_gate, w_up, w_down):
    logits = (x @ w_router).astype(jnp.float32)
    top_v, top_i = lax.top_k(logits, TOP_K)
    wts = jax.nn.softmax(top_v, axis=-1)
    combine = jnp.einsum('btk,btke->bte', wts, jax.nn.one_hot(top_i, N_EXPERTS, dtype=jnp.float32))
    y = jnp.zeros_like(x)
    for e in range(N_EXPERTS):
        y = y + combine[..., e:e + 1].astype(x.dtype) * _swiglu(x, w_gate[e], w_up[e], w_down[e])
    return y


def _gla_recurrence(q, k, v, log_a, s0):
    B, T, H, _ = q.shape
    DV = v.shape[-1]
    C = min(GLA_CHUNK, T)
    nc = -(-T // C)

    def chunks(a):
        a = _pad_axis(a.astype(jnp.float32), 1, nc * C)
        return a.reshape(B, nc, C, H, a.shape[-1])

    q, k, v, log_a = chunks(q), chunks(k), chunks(v), chunks(log_a)
    b = jnp.cumsum(log_a, axis=2)
    b_mid = b[:, :, (C - 1) // 2][:, :, None]
    b_last = b[:, :, C - 1]
    a_intra = jnp.einsum('bnihk,bnjhk->bnhij', q * jnp.exp(b - b_mid), k * jnp.exp(b_mid - b))
    causal = jnp.tril(jnp.ones((C, C), dtype=bool))
    o_intra = jnp.einsum('bnhij,bnjhv->bnihv', jnp.where(causal, a_intra, 0.0), v)
    q_dec = q * jnp.exp(b)
    k_dec = k * jnp.exp(b_last[:, :, None] - b)
    a_last = jnp.exp(b_last)

    def step(S, inp):
        qd, kd, vc, al = inp
        o = jnp.einsum('bihk,bhkv->bihv', qd, S)
        S = al[..., None] * S + jnp.einsum('bjhk,bjhv->bhkv', kd, vc)
        return S, o

    xs = tuple(jnp.moveaxis(a, 1, 0) for a in (q_dec, k_dec, v, a_last))
    s_fin, o_inter = lax.scan(step, s0.astype(jnp.float32), xs)
    o = o_intra + jnp.moveaxis(o_inter, 0, 1)
    return o.reshape(B, nc * C, H, DV)[:, :T], s_fin


def _gla_mixer(x, s0, w_in, w_a2, b_a, norm_g, w_o):
    B, T, _ = x.shape
    hk, hv = GLA_HEADS * GLA_DK, GLA_HEADS * GLA_DV
    proj = x @ w_in
    q, k, v, r, a = jnp.split(proj, [hk, 2 * hk, 2 * hk + hv, 2 * hk + 2 * hv], axis=-1)
    q = q.reshape(B, T, GLA_HEADS, GLA_DK) * GLA_DK ** -0.5
    k = k.reshape(B, T, GLA_HEADS, GLA_DK)
    v = v.reshape(B, T, GLA_HEADS, GLA_DV)
    log_a = jax.nn.log_sigmoid((a @ w_a2 + b_a).astype(jnp.float32)).reshape(B, T, GLA_HEADS, GLA_DK) / GLA_TAU
    o, s_new = _gla_recurrence(q, k, v, log_a, s0)
    o = o * lax.rsqrt(jnp.square(o).mean(-1, keepdims=True) + LN_EPS) * norm_g
    o = o * jax.nn.silu(r.reshape(B, T, GLA_HEADS, GLA_DV).astype(jnp.float32))
    return o.reshape(B, T, hv).astype(x.dtype) @ w_o, s_new


def _compress(rows, pe, w1, w2):
    B, L = rows.shape[:2]
    n_cmp = L // CMP_STRIDE - 1
    n_ch = -(-L // CMP_STRIDE)
    r = _pad_axis(rows, 1, n_ch * CMP_STRIDE).reshape(B, n_ch, CMP_STRIDE, NSA_GROUPS, NSA_HD)
    r = jnp.moveaxis(r, 3, 2).reshape(B, n_ch, NSA_GROUPS, CMP_STRIDE * NSA_HD)
    pe2 = pe.reshape(2, CMP_STRIDE * NSA_HD)
    w1r = w1.reshape(2, CMP_STRIDE * NSA_HD, CMP_HIDDEN)
    h = (r[:, :n_cmp] + pe2[0]) @ w1r[0] + (r[:, 1:n_cmp + 1] + pe2[1]) @ w1r[1]
    return jax.nn.gelu(h) @ w2


def _gather_sel(idx, new_blk, n_past_blk, pool_blk, page_table):
    bi = jnp.arange(idx.shape[0])[:, None, None, None]
    gi = jnp.arange(NSA_GROUPS)[None, None, :, None]
    kv = new_blk[bi, jnp.clip(idx - n_past_blk, 0, new_blk.shape[1] - 1), :, :, gi, :]
    if pool_blk is None:
        return kv
    spp = PAGE_SIZE // SEL_BLOCK
    jp = jnp.clip(idx, 0, n_past_blk - 1)
    past = pool_blk[page_table[bi, jp // spp], jp % spp, :, :, gi, :]
    return jnp.where((idx < n_past_blk)[..., None, None, None], past, kv)


def _nsa_shared_kv(s, qpos, past, w_kv, pe_k, w1_k, w2_k, pe_v, w1_v, w2_v):
    B, T, _ = s.shape
    rows = (s @ w_kv).reshape(B, T, 6, NSA_GROUPS, NSA_HD)
    cmp_rows = rows[:, :, 0:2]
    sel_rows = jnp.stack([_rope(rows[:, :, 2], qpos), rows[:, :, 3]], axis=2)
    win_rows = jnp.stack([_rope(rows[:, :, 4], qpos), rows[:, :, 5]], axis=2)
    if past is None:
        cmp_all = cmp_rows
        win_new = win_rows[:, T - min(WINDOW, T):]
        win_keys = _pad_front(win_rows, WINDOW)
        pool_blk, page_table, n_past_blk = None, None, 0
    else:
        cache_cmp, cache_sel, page_table, state_win = past
        n_past = page_table.shape[1] * PAGE_SIZE
        cmp_past = cache_cmp[page_table].reshape(B, n_past, 2, NSA_GROUPS, NSA_HD)
        cmp_all = jnp.concatenate([cmp_past, cmp_rows], axis=1)
        pool_blk = cache_sel.reshape(cache_sel.shape[0], PAGE_SIZE // SEL_BLOCK, SEL_BLOCK, 2, NSA_GROUPS, NSA_HD)
        n_past_blk = n_past // SEL_BLOCK
        wb = state_win.shape[1]
        win_cat = jnp.concatenate([state_win, win_rows], axis=1)
        win_new = win_cat[:, T:]
        win_keys = _pad_front(win_cat, WINDOW - wb)
    ck = _compress(cmp_all[:, :, 0], pe_k, w1_k, w2_k)
    cv = _compress(cmp_all[:, :, 1], pe_v, w1_v, w2_v)
    cstart = jnp.arange(ck.shape[1], dtype=jnp.int32) * CMP_STRIDE
    ck = _rope(ck, cstart)
    cend = cstart + CMP_LEN - 1
    n_new_blk = -(-T // SEL_BLOCK)
    new_blk = _pad_axis(sel_rows, 1, n_new_blk * SEL_BLOCK).reshape(B, n_new_blk, SEL_BLOCK, 2, NSA_GROUPS, NSA_HD)
    gather_fn = functools.partial(_gather_sel, new_blk=new_blk, n_past_blk=n_past_blk, pool_blk=pool_blk, page_table=page_table)
    shared = (ck, cv, cend, gather_fn, n_past_blk + n_new_blk, win_keys)
    return shared, cmp_rows, sel_rows, win_new


def _nsa_mixer(x, qpos, pos0, shared, w_in, w_o):
    ck, cv, cend, gather_fn, n_sel, win_keys = shared
    B, T, _ = x.shape
    proj = x @ w_in
    q = proj[..., :NSA_HEADS * NSA_HD].reshape(B, T, NSA_HEADS, NSA_HD)
    gates = jax.nn.sigmoid(proj[..., NSA_HEADS * NSA_HD:].astype(jnp.float32)).reshape(B, T, 3, NSA_GROUPS, NSA_HPG, 1)
    q = (_rope(q, qpos) * NSA_HD ** -0.5).reshape(B, T, NSA_GROUPS, NSA_HPG, NSA_HD)

    s_c = jnp.einsum('btghd,bngd->btghn', q, ck, preferred_element_type=jnp.float32)
    p_c = _masked_softmax(s_c, (cend[None, :] <= qpos[:, None])[None, :, None, None, :])
    o_cmp = jnp.einsum('btghn,bngd->btghd', p_c, cv.astype(jnp.float32))

    ci = jnp.arange(ck.shape[1], dtype=jnp.int32) * CMP_STRIDE
    sj = jnp.arange(n_sel, dtype=jnp.int32) * SEL_BLOCK
    overlap = ((ci[:, None] < sj[None, :] + SEL_BLOCK) & (ci[:, None] + CMP_LEN > sj[None, :])).astype(jnp.float32)
    imp = jnp.einsum('btgn,nj->btgj', p_c.sum(3), overlap)
    jj = jnp.arange(n_sel, dtype=jnp.int32)[None, :]
    cur = (qpos // SEL_BLOCK)[:, None]
    valid = (sj[None, :] <= qpos[:, None])[None, :, None, :]
    forced = ((jj == 0) | (jj == cur) | (jj == cur - 1))[None, :, None, :]
    score = jnp.where(valid, jnp.where(forced, FORCE_SCORE, imp), -FORCE_SCORE)
    _, idx = lax.top_k(score, min(SEL_TOP, n_sel))

    def sel_block(i, q_b, idx_b):
        qb = q_b.shape[1]
        qpos_b = pos0 + i * qb + jnp.arange(qb, dtype=jnp.int32)
        kv = gather_fn(idx_b)
        kpos = idx_b[..., None] * SEL_BLOCK + jnp.arange(SEL_BLOCK, dtype=jnp.int32)
        mask = (kpos <= qpos_b[None, :, None, None, None]).reshape(B, qb, NSA_GROUPS, 1, -1)
        s = jnp.einsum('bqghd,bqgksd->bqghks', q_b, kv[..., 0, :], preferred_element_type=jnp.float32)
        p = _masked_softmax(s.reshape(B, qb, NSA_GROUPS, NSA_HPG, -1), mask)
        vv = kv[..., 1, :].reshape(B, qb, NSA_GROUPS, -1, NSA_HD).astype(jnp.float32)
        return jnp.einsum('bqghn,bqgnd->bqghd', p, vv)

    o_sel = _blocked_map(sel_block, min(SEL_Q_BLOCK, T), q, idx)

    qbw = min(WIN_Q_BLOCK, T)
    win_keys = _pad_axis(win_keys, 1, WINDOW + (-(-T // qbw)) * qbw)

    def win_block(i, q_b):
        qb = q_b.shape[1]
        kv = lax.dynamic_slice_in_dim(win_keys, i * qb, WINDOW + qb, axis=1)
        kpos = pos0 - WINDOW + i * qb + jnp.arange(WINDOW + qb, dtype=jnp.int32)
        qpos_b = pos0 + i * qb + jnp.arange(qb, dtype=jnp.int32)
        mask = (kpos[None, :] <= qpos_b[:, None]) & (kpos[None, :] > qpos_b[:, None] - WINDOW) & (kpos[None, :] >= 0)
        s = jnp.einsum('bqghd,bkgd->bqghk', q_b, kv[:, :, 0], preferred_element_type=jnp.float32)
        p = _masked_softmax(s, mask[None, :, None, None, :])
        return jnp.einsum('bqghk,bkgd->bqghd', p, kv[:, :, 1].astype(jnp.float32))

    o_win = _blocked_map(win_block, qbw, q)

    o = gates[:, :, 0] * o_cmp + gates[:, :, 1] * o_sel + gates[:, :, 2] * o_win
    return o.reshape(B, T, NSA_HEADS * NSA_HD).astype(x.dtype) @ w_o


def _trunk(x, ple, pos0, gla_s0, past, w):
    B, T, _ = x.shape
    qpos = pos0 + jnp.arange(T, dtype=jnp.int32)
    gla_states = []
    shared, cmp_rows, sel_rows, win_new = None, None, None, None
    for i in range(DEPTH):
        if i < N_A_LAYERS:
            h, s_new = _gla_mixer(x, gla_s0[i], w['gla_w_in'][i], w['gla_w_a2'][i], w['gla_b_a'][i],
                                  w['gla_norm_g'][i], w['gla_w_o'][i])
            gla_states.append(s_new)
        else:
            if i == N_A_LAYERS:
                shared, cmp_rows, sel_rows, win_new = _nsa_shared_kv(
                    x, qpos, past, w['nsa_w_kv'], w['cmp_pe_k'], w['cmp_w1_k'], w['cmp_w2_k'],
                    w['cmp_pe_v'], w['cmp_w1_v'], w['cmp_w2_v'])
            j = i - N_A_LAYERS
            h = _nsa_mixer(x, qpos, pos0, shared, w['nsa_w_in'][j], w['nsa_w_o'][j])
        x = _layer_norm(DEEPNORM_ALPHA * x + h, w['ln_g'][i, 0], w['ln_b'][i, 0])
        if i % 2 == 0:
            f = _swiglu(x, w['ffn_w_gate'][i // 2], w['ffn_w_up'][i // 2], w['ffn_w_down'][i // 2])
        else:
            f = _moe(x, w['moe_w_router'][i // 2], w['moe_w_gate'][i // 2], w['moe_w_up'][i // 2], w['moe_w_down'][i // 2])
        x = _layer_norm(DEEPNORM_ALPHA * x + f, w['ln_g'][i, 1], w['ln_b'][i, 1])
        x = x + jax.nn.sigmoid(x @ w['ple_w_gate'][i]) * (ple[i] @ w['ple_w_proj'][i])
    return x, jnp.stack(gla_states), cmp_rows, sel_rows, win_new


def setup_inputs(seed: int = 0) -> dict:
    key = jax.random.key(seed)
    ks = iter(jax.random.split(key, 48))

    def nrm(shape, scale=1.0):
        return scale * jax.random.normal(next(ks), shape, jnp.float32)

    d = D_MODEL
    n_pages = PAST_LEN // PAGE_SIZE
    n_used = DEC_BATCH * n_pages
    n_pool = n_used + max(1, n_used // 4)
    wb = min(WINDOW, PAST_LEN)
    beta = DEEPNORM_BETA
    return {
        'x_prompt': nrm((BATCH, SEQ, d)),
        'x_sample': nrm((DEC_BATCH, DEC_SEQ, d)),
        'state_gla': nrm((N_A_LAYERS, DEC_BATCH, GLA_HEADS, GLA_DK, GLA_DV), 0.5),
        'cache_cmp': nrm((n_pool, PAGE_SIZE, 2, NSA_GROUPS, NSA_HD)),
        'cache_sel': nrm((n_pool, PAGE_SIZE, 2, NSA_GROUPS, NSA_HD)),
        'state_win': nrm((DEC_BATCH, wb, 2, NSA_GROUPS, NSA_HD)),
        'page_table': jax.random.permutation(next(ks), n_pool)[:n_used].reshape(DEC_BATCH, n_pages).astype(jnp.int32),
        'p_prompt': nrm((DEPTH, BATCH, SEQ, PLE_DIM)),
        'p_sample': nrm((DEPTH, DEC_BATCH, DEC_SEQ, PLE_DIM)),
        'ln_g': 1.0 + nrm((DEPTH, 2, d), 0.02),
        'ln_b': nrm((DEPTH, 2, d), 0.02),
        'gla_w_in': nrm((N_A_LAYERS, d, GLA_IN), d ** -0.5),
        'gla_w_a2': nrm((N_A_LAYERS, GLA_GATE_RANK, GLA_HEADS * GLA_DK), GLA_GATE_RANK ** -0.5),
        'gla_b_a': nrm((N_A_LAYERS, GLA_HEADS * GLA_DK), 0.1),
        'gla_norm_g': 1.0 + nrm((N_A_LAYERS, GLA_DV), 0.02),
        'gla_w_o': nrm((N_A_LAYERS, GLA_HEADS * GLA_DV, d), (GLA_HEADS * GLA_DV) ** -0.5 * beta),
        'nsa_w_kv': nrm((d, NSA_KV), d ** -0.5),
        'cmp_pe_k': nrm((CMP_LEN, NSA_HD), 0.1),
        'cmp_w1_k': nrm((CMP_LEN * NSA_HD, CMP_HIDDEN), (CMP_LEN * NSA_HD) ** -0.5),
        'cmp_w2_k': nrm((CMP_HIDDEN, NSA_HD), CMP_HIDDEN ** -0.5),
        'cmp_pe_v': nrm((CMP_LEN, NSA_HD), 0.1),
        'cmp_w1_v': nrm((CMP_LEN * NSA_HD, CMP_HIDDEN), (CMP_LEN * NSA_HD) ** -0.5),
        'cmp_w2_v': nrm((CMP_HIDDEN, NSA_HD), CMP_HIDDEN ** -0.5),
        'nsa_w_in': nrm((N_B_LAYERS, d, NSA_IN), d ** -0.5),
        'nsa_w_o': nrm((N_B_LAYERS, NSA_HEADS * NSA_HD, d), (NSA_HEADS * NSA_HD) ** -0.5 * beta),
        'ffn_w_gate': nrm((N_DENSE, d, D_FF_DENSE), d ** -0.5),
        'ffn_w_up': nrm((N_DENSE, d, D_FF_DENSE), d ** -0.5),
        'ffn_w_down': nrm((N_DENSE, D_FF_DENSE, d), D_FF_DENSE ** -0.5 * beta),
        'moe_w_router': nrm((N_MOE, d, N_EXPERTS), d ** -0.5),
        'moe_w_gate': nrm((N_MOE, N_EXPERTS, d, D_FF_EXPERT), d ** -0.5),
        'moe_w_up': nrm((N_MOE, N_EXPERTS, d, D_FF_EXPERT), d ** -0.5),
        'moe_w_down': nrm((N_MOE, N_EXPERTS, D_FF_EXPERT, d), D_FF_EXPERT ** -0.5 * beta),
        'ple_w_proj': nrm((DEPTH, PLE_DIM, d), PLE_DIM ** -0.5),
        'ple_w_gate': nrm((DEPTH, d, d), d ** -0.5),
    }


def reference(x_prompt, x_sample, state_gla, cache_cmp, cache_sel, state_win, page_table, p_prompt, p_sample,
              ln_g, ln_b, gla_w_in, gla_w_a2, gla_b_a, gla_norm_g, gla_w_o,
              nsa_w_kv, cmp_pe_k, cmp_w1_k, cmp_w2_k, cmp_pe_v, cmp_w1_v, cmp_w2_v,
              nsa_w_in, nsa_w_o, ffn_w_gate, ffn_w_up, ffn_w_down,
              moe_w_router, moe_w_gate, moe_w_up, moe_w_down, ple_w_proj, ple_w_gate):
    w = dict(ln_g=ln_g, ln_b=ln_b, gla_w_in=gla_w_in, gla_w_a2=gla_w_a2, gla_b_a=gla_b_a,
             gla_norm_g=gla_norm_g, gla_w_o=gla_w_o, nsa_w_kv=nsa_w_kv,
             cmp_pe_k=cmp_pe_k, cmp_w1_k=cmp_w1_k, cmp_w2_k=cmp_w2_k,
             cmp_pe_v=cmp_pe_v, cmp_w1_v=cmp_w1_v, cmp_w2_v=cmp_w2_v,
             nsa_w_in=nsa_w_in, nsa_w_o=nsa_w_o, ffn_w_gate=ffn_w_gate, ffn_w_up=ffn_w_up,
             ffn_w_down=ffn_w_down, moe_w_router=moe_w_router, moe_w_gate=moe_w_gate,
             moe_w_up=moe_w_up, moe_w_down=moe_w_down, ple_w_proj=ple_w_proj, ple_w_gate=ple_w_gate)
    gla_zero = jnp.zeros((N_A_LAYERS, x_prompt.shape[0], GLA_HEADS, GLA_DK, GLA_DV), jnp.float32)
    y_prompt, gla_state_prompt, cmp_rows_prompt, sel_rows_prompt, win_prompt = _trunk(
        x_prompt, p_prompt, 0, gla_zero, None, w)
    y_sample, gla_state_sample, cmp_rows_sample, sel_rows_sample, win_sample = _trunk(
        x_sample, p_sample, PAST_LEN, state_gla, (cache_cmp, cache_sel, page_table, state_win), w)
    return (y_prompt, y_sample, gla_state_prompt, gla_state_sample, cmp_rows_prompt, cmp_rows_sample,
            sel_rows_prompt, sel_rows_sample, win_prompt, win_sample)
```

```python
import functools
import math

import jax
import jax.numpy as jnp
from jax import lax
from jax.experimental import pallas as pl
from jax.experimental.pallas import tpu as pltpu

F32 = jnp.float32
BF16 = jnp.bfloat16

D_MODEL = 1024
DEPTH = 2
PAGE_SIZE = 128
PLE_DIM = 256
DEEPNORM_ALPHA = (2.0 * DEPTH) ** 0.25
LN_EPS = 1e-5
ROPE_THETA = 10000.0

GLA_HEADS = 4
GLA_DK = 128
GLA_DV = 256
GLA_GATE_RANK = 16
GLA_TAU = 16.0
GLA_CHUNK = 64

NSA_HEADS = 16
NSA_HD = 64
NSA_GROUPS = 2
NSA_HPG = 8
CMP_STRIDE = 16
CMP_LEN = 32
CMP_HIDDEN = 128
SEL_BLOCK = 64
SEL_TOP = 16
WINDOW = 512
NEG_INF = -1e30
FORCE_SCORE = 1e9

N_EXPERTS = 8
LANES = 128
VMEM_LIMIT = 56 * 1024 * 1024

_NT = (((1,), (1,)), ((), ()))
_TN = (((0,), (0,)), ((), ()))


def _cp(*sem):
    return pltpu.CompilerParams(dimension_semantics=sem, vmem_limit_bytes=VMEM_LIMIT)


def _dot(a, b):
    return jnp.dot(a, b, preferred_element_type=F32)


def _dot_nt(a, b):
    return lax.dot_general(a, b, _NT, preferred_element_type=F32)


def _dot_tn(a, b):
    return lax.dot_general(a, b, _TN, preferred_element_type=F32)


def _split2(x):
    hi = x.astype(BF16)
    lo = (x - hi.astype(F32)).astype(BF16)
    return hi, lo


def _split3(x):
    hi = x.astype(BF16)
    r = x - hi.astype(F32)
    mid = r.astype(BF16)
    lo = (r - mid.astype(F32)).astype(BF16)
    return hi, mid, lo


def _vmod(x, n):
    assert n & (n - 1) == 0
    return x & (n - 1)


def _vdiv(x, n):
    assert n & (n - 1) == 0
    return x >> (n.bit_length() - 1)


def _layer_norm(y, g, b):
    mu = jnp.mean(y, axis=-1, keepdims=True)
    yc = y - mu
    var = jnp.mean(yc * yc, axis=-1, keepdims=True)
    return yc * lax.rsqrt(var + LN_EPS) * g + b


def _rope_lanes(x, cos, sin):
    lane = lax.broadcasted_iota(jnp.int32, x.shape, x.ndim - 1)
    first = _vmod(lane, NSA_HD) < (NSA_HD // 2)
    partner = jnp.where(first, pltpu.roll(x, LANES - NSA_HD // 2, x.ndim - 1),
                        pltpu.roll(x, NSA_HD // 2, x.ndim - 1))
    return x * cos + partner * sin


def _proj_body(x_ref, w_ref, o_ref):
    o_ref[...] = _dot(x_ref[...], w_ref[...]).astype(o_ref.dtype)


def _proj(x, w, tm, tn, out_dtype):
    M, K = x.shape
    N = w.shape[1]
    return pl.pallas_call(
        _proj_body, grid=(M // tm, N // tn),
        in_specs=[pl.BlockSpec((tm, K), lambda i, j: (i, 0)),
                  pl.BlockSpec((K, tn), lambda i, j: (0, j))],
        out_specs=pl.BlockSpec((tm, tn), lambda i, j: (i, j)),
        out_shape=jax.ShapeDtypeStruct((M, N), out_dtype),
        compiler_params=_cp("parallel", "arbitrary"), name="proj")(x, w)


def _gla_gate_body(x_ref, wa_ref, wa2_ref, ba_ref, o_ref):
    a = _dot(x_ref[...], wa_ref[...])
    z = _dot(a.astype(BF16), wa2_ref[...]) + ba_ref[...]
    o_ref[...] = (jnp.minimum(z, 0.0) - jnp.log1p(jnp.exp(-jnp.abs(z)))) * (1.0 / GLA_TAU)


def _gla_gate(xb, wa, wa2, ba, tm):
    M = xb.shape[0]
    N = wa2.shape[1]
    return pl.pallas_call(
        _gla_gate_body, grid=(M // tm,),
        in_specs=[pl.BlockSpec((tm, D_MODEL), lambda i: (i, 0)),
                  pl.BlockSpec((D_MODEL, LANES), lambda i: (0, 0)),
                  pl.BlockSpec((LANES, N), lambda i: (0, 0)),
                  pl.BlockSpec((1, N), lambda i: (0, 0))],
        out_specs=pl.BlockSpec((tm, N), lambda i: (i, 0)),
        out_shape=jax.ShapeDtypeStruct((M, N), F32),
        compiler_params=_cp("parallel"), name="gla_gate")(xb, wa, wa2, ba)


def _gla_body(q_ref, k_ref, v_ref, r_ref, la_ref, g_ref, s0_ref, o_ref, sout_ref, s_ref, *, C, NC):
    t = pl.program_id(2)

    @pl.when(t == 0)
    def _():
        s_ref[...] = s0_ref[0, 0]

    row = lax.broadcasted_iota(jnp.int32, (C, C), 0)
    col = lax.broadcasted_iota(jnp.int32, (C, C), 1)
    tri = col <= row
    tri_b = tri.astype(BF16)
    ones = jnp.ones((C, LANES), BF16)
    rowk = lax.broadcasted_iota(jnp.int32, (C, GLA_DK), 0)
    for c in range(NC):
        sl = pl.ds(c * C, C)
        la = la_ref[sl, :]
        hi, mid, lo = _split3(la)
        b = _dot(tri_b, hi) + _dot(tri_b, mid) + _dot(tri_b, lo)
        tot = _dot_tn(hi, ones) + _dot_tn(mid, ones) + _dot_tn(lo, ones)
        b_mid = jnp.sum(jnp.where(rowk == (C - 1) // 2, b, 0.0), axis=0, keepdims=True)
        b_last = jnp.sum(jnp.where(rowk == C - 1, b, 0.0), axis=0, keepdims=True)
        q = q_ref[sl, :] * (GLA_DK ** -0.5)
        k = k_ref[sl, :]
        v = v_ref[sl, :].astype(BF16)
        qe = (q * jnp.exp(b - b_mid)).astype(BF16)
        ke = (k * jnp.exp(b_mid - b)).astype(BF16)
        a = jnp.where(tri, _dot_nt(qe, ke), 0.0)
        s_old = s_ref[...]
        o = _dot(a.astype(BF16), v) + _dot((q * jnp.exp(b)).astype(BF16), s_old.astype(BF16))
        kd = (k * jnp.exp(b_last - b)).astype(BF16)
        s_ref[...] = jnp.exp(tot[:, 0:1]) * s_old + _dot_tn(kd, v)
        ms = jnp.mean(o * o, axis=-1, keepdims=True)
        rr = r_ref[sl, :]
        o = o * lax.rsqrt(ms + LN_EPS) * g_ref[...] * (rr * jax.nn.sigmoid(rr))
        o_ref[sl, :] = o.astype(o_ref.dtype)

    @pl.when(t == pl.num_programs(2) - 1)
    def _():
        sout_ref[0, 0] = s_ref[...]


def _gla(proj, la, norm_g, s0, B, T, C, NC):
    Tt = C * NC
    nT = T // Tt
    M = B * T
    kq = GLA_HEADS * GLA_DK // GLA_DK
    kv = 2 * GLA_HEADS * GLA_DK // GLA_DV
    body = functools.partial(_gla_body, C=C, NC=NC)
    return pl.pallas_call(
        body, grid=(B, GLA_HEADS, nT),
        in_specs=[pl.BlockSpec((Tt, GLA_DK), lambda b, h, t: (b * nT + t, h)),
                  pl.BlockSpec((Tt, GLA_DK), lambda b, h, t: (b * nT + t, kq + h)),
                  pl.BlockSpec((Tt, GLA_DV), lambda b, h, t: (b * nT + t, kv + h)),
                  pl.BlockSpec((Tt, GLA_DV), lambda b, h, t: (b * nT + t, kv + GLA_HEADS + h)),
                  pl.BlockSpec((Tt, GLA_DK), lambda b, h, t: (b * nT + t, h)),
                  pl.BlockSpec((1, GLA_DV), lambda b, h, t: (0, 0)),
                  pl.BlockSpec((1, 1, GLA_DK, GLA_DV), lambda b, h, t: (b, h, 0, 0))],
        out_specs=[pl.BlockSpec((Tt, GLA_DV), lambda b, h, t: (b * nT + t, h)),
                   pl.BlockSpec((1, 1, GLA_DK, GLA_DV), lambda b, h, t: (b, h, 0, 0))],
        out_shape=[jax.ShapeDtypeStruct((M, GLA_HEADS * GLA_DV), BF16),
                   jax.ShapeDtypeStruct((B, GLA_HEADS, GLA_DK, GLA_DV), F32)],
        scratch_shapes=[pltpu.VMEM((GLA_DK, GLA_DV), F32)],
        compiler_params=_cp("parallel", "parallel", "arbitrary"), name="gla")(
            proj, proj, proj, proj, la, norm_g, s0)


def _mm_ln_body(a_ref, w_ref, r_ref, g_ref, b_ref, of_ref, ob_ref):
    y = DEEPNORM_ALPHA * r_ref[...] + _dot(a_ref[...], w_ref[...])
    out = _layer_norm(y, g_ref[...], b_ref[...])
    of_ref[...] = out
    ob_ref[...] = out.astype(BF16)


def _mm_ln(a, w, resid, g, b, tm):
    M, K = a.shape
    N = w.shape[1]
    return pl.pallas_call(
        _mm_ln_body, grid=(M // tm,),
        in_specs=[pl.BlockSpec((tm, K), lambda i: (i, 0)),
                  pl.BlockSpec((K, N), lambda i: (0, 0)),
                  pl.BlockSpec((tm, N), lambda i: (i, 0)),
                  pl.BlockSpec((1, N), lambda i: (0, 0)),
                  pl.BlockSpec((1, N), lambda i: (0, 0))],
        out_specs=[pl.BlockSpec((tm, N), lambda i: (i, 0)),
                   pl.BlockSpec((tm, N), lambda i: (i, 0))],
        out_shape=[jax.ShapeDtypeStruct((M, N), F32), jax.ShapeDtypeStruct((M, N), BF16)],
        compiler_params=_cp("parallel"), name="mm_ln")(a, w, resid, g, b)


def _ffn_body(xb_ref, xf_ref, wg_ref, wu_ref, wd_ref, g_ref, b_ref, of_ref, ob_ref, acc_ref):
    j = pl.program_id(1)

    @pl.when(j == 0)
    def _():
        acc_ref[...] = jnp.zeros_like(acc_ref)

    x = xb_ref[...]
    hg = _dot(x, wg_ref[...])
    hu = _dot(x, wu_ref[...])
    h = (hg * jax.nn.sigmoid(hg)) * hu
    acc_ref[...] += _dot(h.astype(BF16), wd_ref[...])

    @pl.when(j == pl.num_programs(1) - 1)
    def _():
        out = _layer_norm(DEEPNORM_ALPHA * xf_ref[...] + acc_ref[...], g_ref[...], b_ref[...])
        of_ref[...] = out
        ob_ref[...] = out.astype(BF16)


def _ffn(xb, xf, wg, wu, wd, g, b, tm, tf):
    M = xb.shape[0]
    F = wg.shape[1]
    return pl.pallas_call(
        _ffn_body, grid=(M // tm, F // tf),
        in_specs=[pl.BlockSpec((tm, D_MODEL), lambda i, j: (i, 0)),
                  pl.BlockSpec((tm, D_MODEL), lambda i, j: (i, 0)),
                  pl.BlockSpec((D_MODEL, tf), lambda i, j: (0, j)),
                  pl.BlockSpec((D_MODEL, tf), lambda i, j: (0, j)),
                  pl.BlockSpec((tf, D_MODEL), lambda i, j: (j, 0)),
                  pl.BlockSpec((1, D_MODEL), lambda i, j: (0, 0)),
                  pl.BlockSpec((1, D_MODEL), lambda i, j: (0, 0))],
        out_specs=[pl.BlockSpec((tm, D_MODEL), lambda i, j: (i, 0)),
                   pl.BlockSpec((tm, D_MODEL), lambda i, j: (i, 0))],
        out_shape=[jax.ShapeDtypeStruct((M, D_MODEL), F32), jax.ShapeDtypeStruct((M, D_MODEL), BF16)],
        scratch_shapes=[pltpu.VMEM((tm, D_MODEL), F32)],
        compiler_params=_cp("parallel", "arbitrary"), name="ffn")(xb, xf, wg, wu, wd, g, b)


def _ple_body(xb_ref, xf_ref, pb_ref, wg_ref, wp_ref, of_ref, ob_ref):
    gate = jax.nn.sigmoid(_dot(xb_ref[...], wg_ref[...]))
    out = xf_ref[...] + gate * _dot(pb_ref[...], wp_ref[...])
    of_ref[...] = out
    ob_ref[...] = out.astype(BF16)


def _ple(xb, xf, pb, wg, wp, tm):
    M = xb.shape[0]
    return pl.pallas_call(
        _ple_body, grid=(M // tm,),
        in_specs=[pl.BlockSpec((tm, D_MODEL), lambda i: (i, 0)),
                  pl.BlockSpec((tm, D_MODEL), lambda i: (i, 0)),
                  pl.BlockSpec((tm, PLE_DIM), lambda i: (i, 0)),
                  pl.BlockSpec((D_MODEL, D_MODEL), lambda i: (0, 0)),
                  pl.BlockSpec((PLE_DIM, D_MODEL), lambda i: (0, 0))],
        out_specs=[pl.BlockSpec((tm, D_MODEL), lambda i: (i, 0)),
                   pl.BlockSpec((tm, D_MODEL), lambda i: (i, 0))],
        out_shape=[jax.ShapeDtypeStruct((M, D_MODEL), F32), jax.ShapeDtypeStruct((M, D_MODEL), BF16)],
        compiler_params=_cp("parallel"), name="ple")(xb, xf, pb, wg, wp)


def _router_body(x_ref, w_ref, o_ref):
    x = x_ref[...]
    xh, xl = _split2(x)
    w = w_ref[...]
    wh, wl = _split2(w)
    logits = _dot(xh, wh) + _dot(xl, wh) + _dot(xh, wl)
    lane = lax.broadcasted_iota(jnp.int32, logits.shape, 1)
    live = lane < N_EXPERTS
    lg = jnp.where(live, logits, -jnp.inf)
    m1 = jnp.max(lg, axis=-1, keepdims=True)
    i1 = jnp.min(jnp.where(lg == m1, lane, LANES), axis=-1, keepdims=True)
    lg2 = jnp.where(lane == i1, -jnp.inf, lg)
    m2 = jnp.max(lg2, axis=-1, keepdims=True)
    i2 = jnp.min(jnp.where(lg2 == m2, lane, LANES), axis=-1, keepdims=True)
    e2 = jnp.exp(m2 - m1)
    den = 1.0 + e2
    o_ref[...] = jnp.where(lane == i1, 1.0 / den, 0.0) + jnp.where(lane == i2, e2 / den, 0.0)


def _router(xf, w_pad, tm):
    M = xf.shape[0]
    return pl.pallas_call(
        _router_body, grid=(M // tm,),
        in_specs=[pl.BlockSpec((tm, D_MODEL), lambda i: (i, 0)),
                  pl.BlockSpec((D_MODEL, LANES), lambda i: (0, 0))],
        out_specs=pl.BlockSpec((tm, LANES), lambda i: (i, 0)),
        out_shape=jax.ShapeDtypeStruct((M, LANES), F32),
        compiler_params=_cp("parallel"), name="router")(xf, w_pad)


def _moe_body(xb_ref, xf_ref, c_ref, wg_ref, wu_ref, wd_ref, g_ref, b_ref, of_ref, ob_ref, acc_ref):
    e = pl.program_id(1)
    j = pl.program_id(2)

    @pl.when((e == 0) & (j == 0))
    def _():
        acc_ref[...] = jnp.zeros_like(acc_ref)

    x = xb_ref[...]
    hg = _dot(x, wg_ref[...])
    hu = _dot(x, wu_ref[...])
    comb = c_ref[...]
    lane = lax.broadcasted_iota(jnp.int32, comb.shape, 1)
    ce = jnp.sum(jnp.where(lane == e, comb, 0.0), axis=-1, keepdims=True)
    h = (hg * jax.nn.sigmoid(hg)) * hu * ce
    acc_ref[...] += _dot(h.astype(BF16), wd_ref[...])

    @pl.when((e == pl.num_programs(1) - 1) & (j == pl.num_programs(2) - 1))
    def _():
        out = _layer_norm(DEEPNORM_ALPHA * xf_ref[...] + acc_ref[...], g_ref[...], b_ref[...])
        of_ref[...] = out
        ob_ref[...] = out.astype(BF16)


def _moe_dense(xb, xf, comb, wg, wu, wd, g, b, tm, tf):
    M = xb.shape[0]
    E, _, F = wg.shape
    return pl.pallas_call(
        _moe_body, grid=(M // tm, E, F // tf),
        in_specs=[pl.BlockSpec((tm, D_MODEL), lambda i, e, j: (i, 0)),
                  pl.BlockSpec((tm, D_MODEL), lambda i, e, j: (i, 0)),
                  pl.BlockSpec((tm, LANES), lambda i, e, j: (i, 0)),
                  pl.BlockSpec((None, D_MODEL, tf), lambda i, e, j: (e, 0, j)),
                  pl.BlockSpec((None, D_MODEL, tf), lambda i, e, j: (e, 0, j)),
                  pl.BlockSpec((None, tf, D_MODEL), lambda i, e, j: (e, j, 0)),
                  pl.BlockSpec((1, D_MODEL), lambda i, e, j: (0, 0)),
                  pl.BlockSpec((1, D_MODEL), lambda i, e, j: (0, 0))],
        out_specs=[pl.BlockSpec((tm, D_MODEL), lambda i, e, j: (i, 0)),
                   pl.BlockSpec((tm, D_MODEL), lambda i, e, j: (i, 0))],
        out_shape=[jax.ShapeDtypeStruct((M, D_MODEL), F32), jax.ShapeDtypeStruct((M, D_MODEL), BF16)],
        scratch_shapes=[pltpu.VMEM((tm, D_MODEL), F32)],
        compiler_params=_cp("parallel", "arbitrary", "arbitrary"), name="moe")(
            xb, xf, comb, wg, wu, wd, g, b)


def _nsa_kv_body(s_ref, w_ref, cos_ref, sin_ref, cmp_ref, sel_ref, win_ref, *rest, tm, nT, with_kvx):
    rows = _dot(s_ref[...], w_ref[...])
    cos = cos_ref[...]
    sin = sin_ref[...]
    cmp_ref[...] = rows[:, 0:256]
    ks = _rope_lanes(rows[:, 256:384], cos, sin)
    vs = rows[:, 384:512]
    kw = _rope_lanes(rows[:, 512:640], cos, sin)
    vw = rows[:, 640:768]
    sel_ref[:, 0:128] = ks
    sel_ref[:, 128:256] = vs
    win_ref[:, 0:128] = kw
    win_ref[:, 128:256] = vw
    if with_kvx:
        kvx_ref = rest[0]
        lane = lax.broadcasted_iota(jnp.int32, (tm, LANES), 1)
        pos = (pl.program_id(0) % nT) * tm + lax.broadcasted_iota(jnp.int32, (tm, LANES), 0)
        low = lane < NSA_HD
        onehot = jnp.where(lane - NSA_HD == _vdiv(pos, SEL_BLOCK), 1.0, 0.0)
        for g in range(NSA_GROUPS):
            pick = (lambda x: x) if g == 0 else (lambda x: pltpu.roll(x, NSA_HD, 1))
            kvx_ref[g, :, 0:128] = jnp.where(low, pick(ks), onehot).astype(BF16)
            kvx_ref[g, :, 128:256] = jnp.where(low, pick(vs), 0.0).astype(BF16)
            kvx_ref[g, :, 256:384] = jnp.where(low, pick(kw), 0.0).astype(BF16)
            kvx_ref[g, :, 384:512] = jnp.where(low, pick(vw), 0.0).astype(BF16)


def _nsa_kv(sb, w, cos, sin, tm, with_kvx):
    M = sb.shape[0]
    nT = cos.shape[0] // tm
    body = functools.partial(_nsa_kv_body, tm=tm, nT=nT, with_kvx=with_kvx)
    out_specs = [pl.BlockSpec((tm, 256), lambda i: (i, 0))] * 3
    out_shape = [jax.ShapeDtypeStruct((M, 256), F32)] * 3
    if with_kvx:
        out_specs = out_specs + [pl.BlockSpec((NSA_GROUPS, tm, 512), lambda i: (0, i, 0))]
        out_shape = out_shape + [jax.ShapeDtypeStruct((NSA_GROUPS, M, 512), BF16)]
    return pl.pallas_call(
        body, grid=(M // tm,),
        in_specs=[pl.BlockSpec((tm, D_MODEL), lambda i: (i, 0)),
                  pl.BlockSpec((D_MODEL, 768), lambda i: (0, 0)),
                  pl.BlockSpec((tm, LANES), lambda i: (i % nT, 0)),
                  pl.BlockSpec((tm, LANES), lambda i: (i % nT, 0))],
        out_specs=out_specs, out_shape=out_shape,
        compiler_params=_cp("parallel"), name="nsa_kv")(sb, w, cos, sin)


def _nsa_q_body(x_ref, w_ref, cos_ref, sin_ref, o_ref):
    res = _dot(x_ref[...], w_ref[...])
    cos = cos_ref[...]
    sin = sin_ref[...]
    for k in range(2):
        xh = res[:, k * LANES:(k + 1) * LANES]
        o_ref[k] = (_rope_lanes(xh, cos, sin) * (NSA_HD ** -0.5)).astype(o_ref.dtype)


def _nsa_q(xb, w_pad, cos, sin, tm, out_dtype):
    M = xb.shape[0]
    nT = cos.shape[0] // tm
    return pl.pallas_call(
        _nsa_q_body, grid=(M // tm, NSA_HEADS // 2),
        in_specs=[pl.BlockSpec((tm, D_MODEL), lambda i, j: (i, 0)),
                  pl.BlockSpec((D_MODEL, 2 * LANES), lambda i, j: (0, j)),
                  pl.BlockSpec((tm, LANES), lambda i, j: (i % nT, 0)),
                  pl.BlockSpec((tm, LANES), lambda i, j: (i % nT, 0))],
        out_specs=pl.BlockSpec((2, tm, LANES), lambda i, j: (j, i, 0)),
        out_shape=jax.ShapeDtypeStruct((NSA_HEADS, M, LANES), out_dtype),
        compiler_params=_cp("parallel", "arbitrary"), name="nsa_q")(xb, w_pad, cos, sin)


def _nsa_gate_body(x_ref, w_ref, o_ref):
    o_ref[0] = jax.nn.sigmoid(_dot(x_ref[...], w_ref[0]))


def _nsa_gate(xb, w_pad, tm):
    M = xb.shape[0]
    S = w_pad.shape[0]
    return pl.pallas_call(
        _nsa_gate_body, grid=(M // tm, S),
        in_specs=[pl.BlockSpec((tm, D_MODEL), lambda i, s: (i, 0)),
                  pl.BlockSpec((1, D_MODEL, LANES), lambda i, s: (s, 0, 0))],
        out_specs=pl.BlockSpec((1, tm, LANES), lambda i, s: (s, i, 0)),
        out_shape=jax.ShapeDtypeStruct((S, M, LANES), F32),
        compiler_params=_cp("parallel", "arbitrary"), name="nsa_gate")(xb, w_pad)


def _compress_rows(x_ref, n_rows, w1_refs, w2_refs, pe_refs):
    lane = lax.broadcasted_iota(jnp.int32, (n_rows, LANES), 1)
    low = lane < NSA_HD
    out = []
    for kv in range(2):
        w1 = w1_refs[kv][...]
        bias = (_dot(pe_refs[kv][0], w1)[0:1, 0:CMP_HIDDEN]
                + _dot(pe_refs[kv][1], w1)[0:1, CMP_HIDDEN:2 * CMP_HIDDEN])
        g0, g1 = [], []
        for rp in range(CMP_STRIDE // 2):
            o0 = (2 * rp) * 256 + kv * LANES
            o1 = (2 * rp + 1) * 256 + kv * LANES
            p = x_ref[:, o0:o0 + LANES]
            q = x_ref[:, o1:o1 + LANES]
            g0.append(jnp.where(low, p, pltpu.roll(q, NSA_HD, 1)).astype(BF16))
            g1.append(jnp.where(low, pltpu.roll(p, NSA_HD, 1), q).astype(BF16))
        per_g = []
        for cols in (g0, g1):
            ab = _dot(jnp.concatenate(cols, axis=1), w1)
            nxt = pltpu.roll(ab[:, CMP_HIDDEN:2 * CMP_HIDDEN], n_rows - 1, 0)
            h = ab[:, 0:CMP_HIDDEN] + nxt + bias
            per_g.append(_dot(jax.nn.gelu(h).astype(BF16), w2_refs[kv][...]))
        out.append(per_g)
    return out


def _compress_prompt_body(x_ref, w1k, w1v, w2k, w2v, pek, pev, cos_ref, sin_ref, o_ref, *, n_rows):
    c = _compress_rows(x_ref, n_rows, (w1k, w1v), (w2k, w2v), (pek, pev))
    for g in range(NSA_GROUPS):
        o_ref[0, g, 0] = _rope_lanes(c[0][g], cos_ref[...], sin_ref[...]).astype(BF16)
        o_ref[0, g, 1] = c[1][g].astype(BF16)


def _compress_prompt(xflat, cw, cos, sin, B, n_rows):
    body = functools.partial(_compress_prompt_body, n_rows=n_rows)
    const2 = lambda b: (0, 0)
    const3 = lambda b: (0, 0, 0)
    return pl.pallas_call(
        body, grid=(B,),
        in_specs=[pl.BlockSpec((n_rows, 4096), lambda b: (b, 0)),
                  pl.BlockSpec((1024, 256), const2), pl.BlockSpec((1024, 256), const2),
                  pl.BlockSpec((CMP_HIDDEN, LANES), const2), pl.BlockSpec((CMP_HIDDEN, LANES), const2),
                  pl.BlockSpec((2, 16, 1024), const3), pl.BlockSpec((2, 16, 1024), const3),
                  pl.BlockSpec((n_rows, LANES), const2), pl.BlockSpec((n_rows, LANES), const2)],
        out_specs=pl.BlockSpec((1, NSA_GROUPS, 2, n_rows, LANES), lambda b: (b, 0, 0, 0, 0)),
        out_shape=jax.ShapeDtypeStruct((B, NSA_GROUPS, 2, n_rows, LANES), BF16),
        compiler_params=_cp("parallel"), name="compress_prompt")(
            xflat, cw["w1k"], cw["w1v"], cw["w2k"], cw["w2v"], cw["pek"], cw["pev"], cos, sin)


def _compress_sample_body(pt_ref, *refs, n_rows, PG):
    pages = refs[:PG]
    w1k, w1v, w2k, w2v, pek, pev, cos_ref, sin_ref, o_ref, x_ref = refs[PG:]
    j = pl.program_id(1)
    rows_per_page = PAGE_SIZE // CMP_STRIDE
    for k in range(PG):
        x_ref[pl.ds((j * PG + k) * rows_per_page, rows_per_page), :] = pages[k][0]

    @pl.when(j == pl.num_programs(1) - 1)
    def _():
        c = _compress_rows(x_ref, n_rows, (w1k, w1v), (w2k, w2v), (pek, pev))
        ck = c[0][0] + pltpu.roll(c[0][1], NSA_HD, 1)
        cv = c[1][0] + pltpu.roll(c[1][1], NSA_HD, 1)
        o_ref[0, :, 0:128] = _rope_lanes(ck, cos_ref[...], sin_ref[...]).astype(BF16)
        o_ref[0, :, 128:256] = cv.astype(BF16)


def _compress_sample(cache_flat, page_table, cw, cos, sin, PG):
    B, n_pages = page_table.shape
    rows_per_page = PAGE_SIZE // CMP_STRIDE
    n_rows = n_pages * rows_per_page
    body = functools.partial(_compress_sample_body, n_rows=n_rows, PG=PG)
    const2 = lambda b, j, pt: (0, 0)
    const3 = lambda b, j, pt: (0, 0, 0)
    page_specs = [pl.BlockSpec((1, rows_per_page, 4096), functools.partial(
        lambda b, j, pt, k: (pt[b, j * PG + k], 0, 0), k=k)) for k in range(PG)]
    grid_spec = pltpu.PrefetchScalarGridSpec(
        num_scalar_prefetch=1, grid=(B, n_pages // PG),
        in_specs=page_specs + [
            pl.BlockSpec((1024, 256), const2), pl.BlockSpec((1024, 256), const2),
            pl.BlockSpec((CMP_HIDDEN, LANES), const2), pl.BlockSpec((CMP_HIDDEN, LANES), const2),
            pl.BlockSpec((2, 16, 1024), const3), pl.BlockSpec((2, 16, 1024), const3),
            pl.BlockSpec((n_rows, LANES), const2), pl.BlockSpec((n_rows, LANES), const2)],
        out_specs=pl.BlockSpec((1, n_rows, 256), lambda b, j, pt: (b, 0, 0)),
        scratch_shapes=[pltpu.VMEM((n_rows, 4096), F32)])
    return pl.pallas_call(
        body, grid_spec=grid_spec,
        out_shape=jax.ShapeDtypeStruct((B, n_rows, 256), BF16),
        compiler_params=_cp("parallel", "arbitrary"), name="compress_sample")(
            page_table, *([cache_flat] * PG), cw["w1k"], cw["w1v"], cw["w2k_lo"], cw["w2v_lo"],
            cw["pek"], cw["pev"], cos, sin)


def _softmax_update(s, m, l, acc, v):
    m_new = jnp.maximum(m, jnp.max(s, axis=-1, keepdims=True))
    a = jnp.exp(m - m_new)
    p = jnp.exp(s - m_new)
    l = a * l + jnp.sum(p, axis=-1, keepdims=True)
    acc = a * acc + _dot(p.astype(BF16), v)
    return m_new, l, acc


def _nsa_attn_body(q_ref, gate_ref, ckv_ref, ov_ref, kvx_ref, o_ref, *, Tq, Tk, n_cmp, n_sel):
    q0 = pl.program_id(2) * Tq
    R = NSA_HPG * Tq
    q3 = q_ref[...]
    qh = q3.reshape(R, LANES)
    trow = q0 + _vmod(lax.broadcasted_iota(jnp.int32, (R, 1), 0), Tq)
    lane = lax.broadcasted_iota(jnp.int32, (1, LANES), 1)

    ck = ckv_ref[0, 0, 0]
    cv = ckv_ref[0, 0, 1]
    s = _dot_nt(qh, ck)
    cmask = (lane < n_cmp) & (CMP_STRIDE * lane + (CMP_LEN - 1) <= trow)
    s = jnp.where(cmask, s, NEG_INF)
    e = jnp.where(cmask, jnp.exp(s - jnp.max(s, axis=-1, keepdims=True)), 0.0)
    p = e / jnp.maximum(jnp.sum(e, axis=-1, keepdims=True), 1e-30)
    o_cmp = _dot(p.astype(BF16), cv)
    psum = jnp.sum(p.reshape(NSA_HPG, Tq, LANES), axis=0)
    ph, plo = _split2(psum)
    imp = _dot(ph, ov_ref[...]) + _dot(plo, ov_ref[...])

    t1 = q0 + lax.broadcasted_iota(jnp.int32, (Tq, 1), 0)
    jj = lane - NSA_HD
    in_range = (jj >= 0) & (jj < n_sel)
    cur = _vdiv(t1, SEL_BLOCK)
    forced = (jj == 0) | (jj == cur) | (jj == cur - 1)
    valid = SEL_BLOCK * jj <= t1
    score = jnp.where(valid, jnp.where(forced, FORCE_SCORE, imp), -FORCE_SCORE)
    rank = jnp.zeros((Tq, LANES), F32)
    for jp in range(n_sel):
        c = score[:, NSA_HD + jp:NSA_HD + jp + 1]
        beats = (c > score) | ((c == score) & (jp < jj))
        rank = rank + jnp.where(beats, 1.0, 0.0)
    keep = rank < float(min(SEL_TOP, n_sel))
    bias = jnp.where(in_range & jnp.logical_not(keep), NEG_INF, 0.0)
    qa = (q3.astype(F32) + bias[None]).astype(BF16).reshape(R, LANES)

    def sel_step(kt, carry):
        k0 = pl.multiple_of(kt * Tk, Tk)
        ka = kvx_ref[0, pl.ds(k0, Tk), 0:128]
        va = kvx_ref[0, pl.ds(k0, Tk), 128:256]
        kpos = k0 + lax.broadcasted_iota(jnp.int32, (1, Tk), 1)
        sc = jnp.where(kpos <= trow, _dot_nt(qa, ka), NEG_INF)
        return _softmax_update(sc, *carry, va)

    init = (jnp.full((R, 1), NEG_INF, F32), jnp.zeros((R, 1), F32), jnp.zeros((R, LANES), F32))
    _, l_s, acc_s = lax.fori_loop(0, (q0 + Tq + Tk - 1) // Tk, sel_step, init)
    o_sel = acc_s / l_s

    def win_step(kt, carry):
        k0 = pl.multiple_of(kt * Tk, Tk)
        kw = kvx_ref[0, pl.ds(k0, Tk), 256:384]
        vw = kvx_ref[0, pl.ds(k0, Tk), 384:512]
        kpos = k0 + lax.broadcasted_iota(jnp.int32, (1, Tk), 1)
        ok = (kpos <= trow) & (kpos > trow - WINDOW)
        sc = jnp.where(ok, _dot_nt(qh, kw), NEG_INF)
        return _softmax_update(sc, *carry, vw)

    lo_t = jnp.maximum(q0 - WINDOW + 1, 0) // Tk
    _, l_w, acc_w = lax.fori_loop(lo_t, (q0 + Tq + Tk - 1) // Tk, win_step, init)
    o_win = acc_w / l_w

    gates = gate_ref[0]
    heads = []
    for h in range(NSA_HPG):
        rs = slice(h * Tq, (h + 1) * Tq)
        heads.append(gates[:, h:h + 1] * o_cmp[rs]
                     + gates[:, NSA_HPG + h:NSA_HPG + h + 1] * o_sel[rs]
                     + gates[:, 2 * NSA_HPG + h:2 * NSA_HPG + h + 1] * o_win[rs])
    low = lane < NSA_HD
    for hp in range(NSA_HPG // 2):
        pair = jnp.where(low, heads[2 * hp], pltpu.roll(heads[2 * hp + 1], NSA_HD, 1))
        o_ref[:, hp * LANES:(hp + 1) * LANES] = pair.astype(o_ref.dtype)


def _nsa_attn_prompt(q, gates, ckv, ov, kvx, B, T, Tq, Tk, n_cmp, n_sel):
    nq = T // Tq
    M = B * T
    n_rows = ckv.shape[3]
    body = functools.partial(_nsa_attn_body, Tq=Tq, Tk=Tk, n_cmp=n_cmp, n_sel=n_sel)
    return pl.pallas_call(
        body, grid=(B, NSA_GROUPS, nq),
        in_specs=[pl.BlockSpec((NSA_HPG, Tq, LANES), lambda b, g, i: (g, b * nq + i, 0)),
                  pl.BlockSpec((1, Tq, LANES), lambda b, g, i: (g, b * nq + i, 0)),
                  pl.BlockSpec((1, 1, 2, n_rows, LANES), lambda b, g, i: (b, g, 0, 0, 0)),
                  pl.BlockSpec((n_rows, LANES), lambda b, g, i: (0, 0)),
                  pl.BlockSpec((1, T, 512), lambda b, g, i: (g, b, 0))],
        out_specs=pl.BlockSpec((Tq, NSA_HPG * NSA_HD), lambda b, g, i: (b * nq + i, g)),
        out_shape=jax.ShapeDtypeStruct((M, NSA_HEADS * NSA_HD), BF16),
        compiler_params=_cp("parallel", "parallel", "arbitrary"), name="nsa_attn_prompt")(
            q, gates, ckv, ov, kvx)


def _col_softmax_update(s, m, l, acc, v):
    m_new = jnp.maximum(m, jnp.max(s, axis=0, keepdims=True))
    a = jnp.exp(m - m_new)
    p = jnp.exp(s - m_new)
    l = a * l + jnp.sum(p, axis=0, keepdims=True)
    return m_new, l, a, _dot_tn(p.astype(BF16), v)


def _nsa_attn_sample_body(pt_ref, *refs, T, PG, past_len, n_cmp, n_sel, n_rows, wb):
    pages = refs[:PG]
    (q_ref, gate_ref, ckv_ref, ovt_ref, swin_ref, wnew_ref, snew_ref, o_ref,
     qb_ref, selb_ref, m_ref, l_ref, acc_ref, oc_ref, ow_ref) = refs[PG:]
    j = pl.program_id(1)
    NQ = NSA_HEADS * T
    lane = lax.broadcasted_iota(jnp.int32, (1, NQ), 1)
    tl = past_len + _vmod(lane, T)
    eye = (lax.broadcasted_iota(jnp.int32, (NQ, NQ), 0) == lax.broadcasted_iota(jnp.int32, (NQ, NQ), 1))

    def to_rows(x):
        return jnp.sum(jnp.where(eye, x, 0.0), axis=1, keepdims=True)

    @pl.when(j == 0)
    def _():
        qp = q_ref[...].reshape(NQ, LANES)
        rowi = lax.broadcasted_iota(jnp.int32, (NQ, LANES), 0)
        qb = jnp.where(rowi < NQ // 2, qp, pltpu.roll(qp, NSA_HD, 1)).astype(BF16)
        qb_ref[...] = qb

        ck = ckv_ref[0, :, 0:128]
        cv = ckv_ref[0, :, 128:256]
        s = _dot_nt(ck, qb)
        nidx = lax.broadcasted_iota(jnp.int32, (n_rows, 1), 0)
        cmask = (nidx < n_cmp) & (CMP_STRIDE * nidx + (CMP_LEN - 1) <= tl)
        s = jnp.where(cmask, s, NEG_INF)
        e = jnp.where(cmask, jnp.exp(s - jnp.max(s, axis=0, keepdims=True)), 0.0)
        p = e / jnp.maximum(jnp.sum(e, axis=0, keepdims=True), 1e-30)
        oc_ref[...] = _dot_tn(p.astype(BF16), cv)
        ri = lax.broadcasted_iota(jnp.int32, (NQ, NQ), 0)
        ci = lax.broadcasted_iota(jnp.int32, (NQ, NQ), 1)
        same = ((_vdiv(ri, NQ // 2) == _vdiv(ci, NQ // 2)) & (_vmod(ri, T) == _vmod(ci, T))).astype(BF16)
        ph, plo = _split2(p)
        psum = _dot(ph, same) + _dot(plo, same)
        sh, slo = _split2(psum)
        imp = _dot(ovt_ref[...], sh) + _dot(ovt_ref[...], slo)

        nb = imp.shape[0]
        jj = lax.broadcasted_iota(jnp.int32, (nb, 1), 0)
        cur = _vdiv(tl, SEL_BLOCK)
        forced = (jj == 0) | (jj == cur) | (jj == cur - 1)
        valid = SEL_BLOCK * jj <= tl
        score = jnp.where(valid, jnp.where(forced, FORCE_SCORE, imp), -FORCE_SCORE)
        work = jnp.where(jj < n_sel, score, -2e38)
        keep = jnp.zeros(work.shape, jnp.bool_)
        for _ in range(min(SEL_TOP, n_sel)):
            mx = jnp.max(work, axis=0, keepdims=True)
            first = jnp.min(jnp.where(work == mx, jj, nb), axis=0, keepdims=True)
            pick = jj == first
            keep = keep | pick
            work = jnp.where(pick, -3e38, work)
        selb_ref[...] = jnp.where(keep, 0.0, NEG_INF)

        kw = swin_ref[0, :, 0:128].astype(BF16)
        vw = swin_ref[0, :, 128:256].astype(BF16)
        sw = _dot_nt(kw, qb)
        kpos = past_len - wb + lax.broadcasted_iota(jnp.int32, (wb, 1), 0)
        sw = jnp.where((kpos <= tl) & (kpos > tl - WINDOW), sw, NEG_INF)
        zero = jnp.zeros((T, 256), F32)
        wn = jnp.concatenate([wnew_ref[...], zero], axis=0)
        sn = _dot_nt(wn[:, 0:128].astype(BF16), qb)
        kposn = past_len + lax.broadcasted_iota(jnp.int32, (2 * T, 1), 0)
        sn = jnp.where((kposn <= tl) & (kposn > tl - WINDOW), sn, NEG_INF)
        mw = jnp.maximum(jnp.max(sw, axis=0, keepdims=True), jnp.max(sn, axis=0, keepdims=True))
        pw = jnp.exp(sw - mw)
        pn = jnp.exp(sn - mw)
        lw = jnp.sum(pw, axis=0, keepdims=True) + jnp.sum(pn, axis=0, keepdims=True)
        ow = _dot_tn(pw.astype(BF16), vw) + _dot_tn(pn.astype(BF16), wn[:, 128:256].astype(BF16))
        ow_ref[...] = ow / to_rows(lw)

        m_ref[...] = jnp.full(m_ref.shape, NEG_INF, F32)
        l_ref[...] = jnp.zeros(l_ref.shape, F32)
        acc_ref[...] = jnp.zeros(acc_ref.shape, F32)

    qb = qb_ref[...]
    kt = jnp.concatenate([pg[0, :, 0:128].astype(BF16) for pg in pages], axis=0)
    vt = jnp.concatenate([pg[0, :, 128:256].astype(BF16) for pg in pages], axis=0)
    blk_per_step = PG * PAGE_SIZE // SEL_BLOCK
    sb = selb_ref[pl.ds(pl.multiple_of(j * blk_per_step, blk_per_step), blk_per_step), :]
    sb = jnp.broadcast_to(sb[:, None, :], (blk_per_step, SEL_BLOCK, NQ)).reshape(PG * PAGE_SIZE, NQ)
    sc = _dot_nt(kt, qb) + sb
    m_new, l_new, a, pv = _col_softmax_update(sc, m_ref[0:1, :], l_ref[0:1, :], None, vt)
    m_ref[0:1, :] = m_new
    l_ref[0:1, :] = l_new
    acc_ref[...] = to_rows(a) * acc_ref[...] + pv

    @pl.when(j == pl.num_programs(1) - 1)
    def _():
        zero = jnp.zeros((T, 256), F32)
        sn_rows = jnp.concatenate([snew_ref[...], zero], axis=0)
        kposn = past_len + lax.broadcasted_iota(jnp.int32, (2 * T, 1), 0)
        n_past_blk = past_len // SEL_BLOCK
        sbn = selb_ref[n_past_blk:n_past_blk + 1, :]
        sc = _dot_nt(sn_rows[:, 0:128].astype(BF16), qb) + sbn
        sc = jnp.where(kposn <= tl, sc, NEG_INF)
        m_new, l_new, a, pv = _col_softmax_update(sc, m_ref[0:1, :], l_ref[0:1, :], None,
                                                  sn_rows[:, 128:256].astype(BF16))
        o_sel = (to_rows(a) * acc_ref[...] + pv) / to_rows(l_new)
        gates = gate_ref[...].reshape(NQ, LANES)
        o = gates[:, 0:1] * oc_ref[...] + gates[:, 1:2] * o_sel + gates[:, 2:3] * ow_ref[...]
        low = lax.broadcasted_iota(jnp.int32, (T, LANES), 1) < NSA_HD
        for hp in range(NSA_HEADS // 2):
            a0 = o[(2 * hp) * T:(2 * hp + 1) * T]
            a1 = o[(2 * hp + 1) * T:(2 * hp + 2) * T]
            if 2 * hp < NSA_HPG:
                piece = jnp.where(low, a0, pltpu.roll(a1, NSA_HD, 1))
            else:
                piece = jnp.where(low, pltpu.roll(a0, NSA_HD, 1), a1)
            o_ref[:, hp * LANES:(hp + 1) * LANES] = piece


def _nsa_attn_sample(q, gates, ckv, ovt, cache_sel2, page_table, state_win2, win_new, sel_new, T, PG):
    B, n_pages = page_table.shape
    past_len = n_pages * PAGE_SIZE
    n_rows = ckv.shape[1]
    wb = state_win2.shape[1]
    n_cmp = (past_len + T) // CMP_STRIDE - 1
    n_sel = past_len // SEL_BLOCK + -(-T // SEL_BLOCK)
    nb = ovt.shape[0]
    NQ = NSA_HEADS * T
    body = functools.partial(_nsa_attn_sample_body, T=T, PG=PG, past_len=past_len, n_cmp=n_cmp,
                             n_sel=n_sel, n_rows=n_rows, wb=wb)
    page_specs = [pl.BlockSpec((1, PAGE_SIZE, 256), functools.partial(
        lambda b, j, pt, k: (pt[b, j * PG + k], 0, 0), k=k)) for k in range(PG)]
    grid_spec = pltpu.PrefetchScalarGridSpec(
        num_scalar_prefetch=1, grid=(B, n_pages // PG),
        in_specs=page_specs + [
            pl.BlockSpec((NSA_HEADS, T, LANES), lambda b, j, pt: (0, b, 0)),
            pl.BlockSpec((NSA_HEADS, T, LANES), lambda b, j, pt: (0, b, 0)),
            pl.BlockSpec((1, n_rows, 256), lambda b, j, pt: (b, 0, 0)),
            pl.BlockSpec((nb, n_rows), lambda b, j, pt: (0, 0)),
            pl.BlockSpec((1, wb, 256), lambda b, j, pt: (b, 0, 0)),
            pl.BlockSpec((T, 256), lambda b, j, pt: (b, 0)),
            pl.BlockSpec((T, 256), lambda b, j, pt: (b, 0))],
        out_specs=pl.BlockSpec((T, NSA_HEADS * NSA_HD), lambda b, j, pt: (b, 0)),
        scratch_shapes=[pltpu.VMEM((NQ, LANES), BF16), pltpu.VMEM((nb, NQ), F32),
                        pltpu.VMEM((8, NQ), F32), pltpu.VMEM((8, NQ), F32),
                        pltpu.VMEM((NQ, LANES), F32), pltpu.VMEM((NQ, LANES), F32),
                        pltpu.VMEM((NQ, LANES), F32)])
    return pl.pallas_call(
        body, grid_spec=grid_spec,
        out_shape=jax.ShapeDtypeStruct((B * T, NSA_HEADS * NSA_HD), F32),
        compiler_params=_cp("parallel", "arbitrary"), name="nsa_attn_sample")(
            page_table, *([cache_sel2] * PG), q, gates, ckv, ovt, state_win2, win_new, sel_new)


def _rope_tables(pos, pattern):
    half = NSA_HD // 2
    inv = ROPE_THETA ** (-jnp.arange(half, dtype=F32) / half)
    ang = pos.astype(F32)[:, None] * inv[None, :]
    cos, sin = jnp.cos(ang), jnp.sin(ang)
    zero = jnp.zeros_like(cos)
    cs = [jnp.concatenate([cos, cos], 1) if p else jnp.concatenate([zero, zero], 1) for p in pattern]
    sn = [jnp.concatenate([-sin, sin], 1) if p else jnp.concatenate([zero, zero], 1) for p in pattern]
    return jnp.concatenate(cs, 1), jnp.concatenate(sn, 1)


def _overlap(n_rows, n_sel):
    ci = jnp.arange(n_rows, dtype=jnp.int32)[:, None] * CMP_STRIDE
    sj = jnp.arange(n_sel, dtype=jnp.int32)[None, :] * SEL_BLOCK
    return ((ci < sj + SEL_BLOCK) & (ci + CMP_LEN > sj)).astype(F32)


def _prep_weights(w):
    d = D_MODEL
    hk, hv = GLA_HEADS * GLA_DK, GLA_HEADS * GLA_DV
    p = {}
    p["ln_g"] = w["ln_g"].reshape(DEPTH, 2, 1, d)
    p["ln_b"] = w["ln_b"].reshape(DEPTH, 2, 1, d)
    w_in = w["gla_w_in"][0]
    p["gla_w_qkvr"] = w_in[:, :2 * hk + 2 * hv].astype(BF16)
    p["gla_w_a"] = jnp.pad(w_in[:, 2 * hk + 2 * hv:], ((0, 0), (0, LANES - GLA_GATE_RANK))).astype(BF16)
    p["gla_w_a2"] = jnp.pad(w["gla_w_a2"][0], ((0, LANES - GLA_GATE_RANK), (0, 0))).astype(BF16)
    p["gla_b_a"] = w["gla_b_a"][0].reshape(1, hk)
    p["gla_norm_g"] = w["gla_norm_g"][0].reshape(1, GLA_DV)
    p["gla_w_o"] = w["gla_w_o"][0].astype(BF16)
    p["nsa_w_kv"] = w["nsa_w_kv"].astype(BF16)
    nw = w["nsa_w_in"][0]
    wq = nw[:, :NSA_HEADS * NSA_HD].reshape(d, NSA_HEADS, NSA_HD)
    p["nsa_w_q"] = jnp.pad(wq, ((0, 0), (0, 0), (0, LANES - NSA_HD))).reshape(d, NSA_HEADS * LANES).astype(BF16)
    wg = nw[:, NSA_HEADS * NSA_HD:].reshape(d, 3, NSA_GROUPS, NSA_HPG)
    wg_grp = jnp.transpose(wg, (2, 0, 1, 3)).reshape(NSA_GROUPS, d, 3 * NSA_HPG)
    p["nsa_w_gate_grp"] = jnp.pad(wg_grp, ((0, 0), (0, 0), (0, LANES - 3 * NSA_HPG))).astype(BF16)
    wg_head = jnp.transpose(wg, (2, 3, 0, 1)).reshape(NSA_HEADS, d, 3)
    p["nsa_w_gate_head"] = jnp.pad(wg_head, ((0, 0), (0, 0), (0, LANES - 3))).astype(BF16)
    p["nsa_w_o"] = w["nsa_w_o"][0].astype(BF16)
    cw = {}
    for name, pe, w1, w2 in (("k", w["cmp_pe_k"], w["cmp_w1_k"], w["cmp_w2_k"]),
                             ("v", w["cmp_pe_v"], w["cmp_w1_v"], w["cmp_w2_v"])):
        half = CMP_STRIDE * NSA_HD
        cw["w1" + name] = jnp.concatenate([w1[:half], w1[half:]], axis=1).astype(BF16)
        cw["w2" + name] = jnp.pad(w2, ((0, 0), (0, LANES - NSA_HD))).astype(BF16)
        cw["w2" + name + "_lo"] = cw["w2" + name]
        pe2 = pe.reshape(2, 1, half)
        cw["pe" + name] = jnp.broadcast_to(pe2, (2, 16, half)).astype(BF16)
    p["cmp"] = cw
    p["ffn_w_gate"] = w["ffn_w_gate"][0].astype(BF16)
    p["ffn_w_up"] = w["ffn_w_up"][0].astype(BF16)
    p["ffn_w_down"] = w["ffn_w_down"][0].astype(BF16)
    p["moe_w_router"] = jnp.pad(w["moe_w_router"][0], ((0, 0), (0, LANES - N_EXPERTS)))
    p["moe_w_gate"] = w["moe_w_gate"][0].astype(BF16)
    p["moe_w_up"] = w["moe_w_up"][0].astype(BF16)
    p["moe_w_down"] = w["moe_w_down"][0].astype(BF16)
    p["ple_w_proj"] = w["ple_w_proj"].astype(BF16)
    p["ple_w_gate"] = w["ple_w_gate"].astype(BF16)
    return p


def _tile(M, cap):
    t = cap
    while M % t:
        t //= 2
    return t


def _trunk(x, ple, pos0, gla_s0, past, p):
    B, T, d = x.shape
    M = B * T
    tm = _tile(M, 1024)
    tm_s = _tile(M, 512)
    xf = x.reshape(M, d)
    xb = xf.astype(BF16)
    pleb = ple.reshape(DEPTH, M, PLE_DIM).astype(BF16)

    proj = _proj(xb, p["gla_w_qkvr"], tm, 512, F32)
    la = _gla_gate(xb, p["gla_w_a"], p["gla_w_a2"], p["gla_b_a"], tm)
    if T >= GLA_CHUNK:
        C, NC, Tp = GLA_CHUNK, _tile(T // GLA_CHUNK, 4), T
    else:
        C, NC, Tp = 16, 1, 16
        pad = lambda a: jnp.pad(a.reshape(B, T, -1), ((0, 0), (0, Tp - T), (0, 0))).reshape(B * Tp, -1)
        proj, la = pad(proj), pad(la)
    o, s_new = _gla(proj, la, p["gla_norm_g"], gla_s0, B, Tp, C, NC)
    if Tp != T:
        o = o.reshape(B, Tp, -1)[:, :T].reshape(M, -1)
    x1f, x1b = _mm_ln(o, p["gla_w_o"], xf, p["ln_g"][0, 0], p["ln_b"][0, 0], tm_s)
    x2f, x2b = _ffn(x1b, x1f, p["ffn_w_gate"], p["ffn_w_up"], p["ffn_w_down"],
                    p["ln_g"][0, 1], p["ln_b"][0, 1], tm, 256)
    x3f, x3b = _ple(x2b, x2f, pleb[0], p["ple_w_gate"][0], p["ple_w_proj"][0], tm_s)

    qpos = pos0 + jnp.arange(T, dtype=jnp.int32)
    cos_kv, sin_kv = _rope_tables(qpos, (1, 1))
    cos_q, sin_q = _rope_tables(qpos, (1, 0))
    if past is None:
        tq = _tile(T, 512)
    else:
        tq = _tile(M, 512)
        cos_kv, sin_kv, cos_q, sin_q = (jnp.tile(a, (B, 1)) for a in (cos_kv, sin_kv, cos_q, sin_q))
    kv = _nsa_kv(x3b, p["nsa_w_kv"], cos_kv, sin_kv, tq, past is None)
    cmp_rows, sel_rows, win_rows = kv[0], kv[1], kv[2]
    if past is None:
        n_ch = T // CMP_STRIDE
        n_cmp = n_ch - 1
        n_sel = T // SEL_BLOCK
        cpos = jnp.arange(n_ch, dtype=jnp.int32) * CMP_STRIDE
        cos_c, sin_c = _rope_tables(cpos, (1, 0))
        ckv = _compress_prompt(cmp_rows.reshape(B * n_ch, CMP_STRIDE * 256), p["cmp"], cos_c, sin_c, B, n_ch)
        ov = jnp.pad(_overlap(n_ch, n_sel), ((0, 0), (NSA_HD, LANES - NSA_HD - n_sel))).astype(BF16)
        q = _nsa_q(x3b, p["nsa_w_q"], cos_q, sin_q, tq, BF16)
        gates = _nsa_gate(x3b, p["nsa_w_gate_grp"], tm_s)
        o = _nsa_attn_prompt(q, gates, ckv, ov, kv[3], B, T, _tile(T, 128), _tile(T, 512), n_cmp, n_sel)
        win_new = win_rows.reshape(B, T, 256)[:, T - min(WINDOW, T):]
    else:
        cache_cmp, cache_sel, page_table, state_win = past
        n_pool = cache_cmp.shape[0]
        n_pages = page_table.shape[1]
        past_len = n_pages * PAGE_SIZE
        PG = _tile(n_pages, 16)
        n_rows = past_len // CMP_STRIDE
        n_sel = past_len // SEL_BLOCK + -(-T // SEL_BLOCK)
        cpos = jnp.arange(n_rows, dtype=jnp.int32) * CMP_STRIDE
        cos_c, sin_c = _rope_tables(cpos, (1, 1))
        ckv = _compress_sample(cache_cmp.reshape(n_pool, PAGE_SIZE // CMP_STRIDE, CMP_STRIDE * 256),
                               page_table, p["cmp"], cos_c, sin_c, PG)
        nb = -(-n_sel // LANES) * LANES
        ovt = jnp.pad(_overlap(n_rows, n_sel).T, ((0, nb - n_sel), (0, 0))).astype(BF16)
        assert T < CMP_STRIDE and T <= SEL_BLOCK
        q = _nsa_q(x3b, p["nsa_w_q"], cos_q, sin_q, tq, F32)
        gates = _nsa_gate(x3b, p["nsa_w_gate_head"], tm_s)
        wb = state_win.shape[1]
        o = _nsa_attn_sample(q, gates, ckv, ovt, cache_sel.reshape(n_pool, PAGE_SIZE, 256), page_table,
                             state_win.reshape(B, wb, 256), win_rows, sel_rows, T, PG).astype(BF16)
        win_new = jnp.concatenate([state_win.reshape(B, wb, 256), win_rows.reshape(B, T, 256)], axis=1)[:, T:]
    x4f, x4b = _mm_ln(o, p["nsa_w_o"], x3f, p["ln_g"][1, 0], p["ln_b"][1, 0], tm_s)
    comb = _router(x4f, p["moe_w_router"], tm_s)
    x5f, x5b = _moe_dense(x4b, x4f, comb, p["moe_w_gate"], p["moe_w_up"], p["moe_w_down"],
                          p["ln_g"][1, 1], p["ln_b"][1, 1], tm, 512)
    x6f, _ = _ple(x5b, x5f, pleb[1], p["ple_w_gate"][1], p["ple_w_proj"][1], tm_s)

    shp = (B, T, 2, NSA_GROUPS, NSA_HD)
    return (x6f.reshape(B, T, d), s_new[None], cmp_rows.reshape(shp), sel_rows.reshape(shp),
            win_new.reshape(B, win_new.shape[1], 2, NSA_GROUPS, NSA_HD))


def kernel(x_prompt, x_sample, state_gla, cache_cmp, cache_sel, state_win, page_table, p_prompt, p_sample,
           ln_g, ln_b, gla_w_in, gla_w_a2, gla_b_a, gla_norm_g, gla_w_o,
           nsa_w_kv, cmp_pe_k, cmp_w1_k, cmp_w2_k, cmp_pe_v, cmp_w1_v, cmp_w2_v,
           nsa_w_in, nsa_w_o, ffn_w_gate, ffn_w_up, ffn_w_down,
           moe_w_router, moe_w_gate, moe_w_up, moe_w_down, ple_w_proj, ple_w_gate):
    w = dict(ln_g=ln_g, ln_b=ln_b, gla_w_in=gla_w_in, gla_w_a2=gla_w_a2, gla_b_a=gla_b_a,
             gla_norm_g=gla_norm_g, gla_w_o=gla_w_o, nsa_w_kv=nsa_w_kv,
             cmp_pe_k=cmp_pe_k, cmp_w1_k=cmp_w1_k, cmp_w2_k=cmp_w2_k,
             cmp_pe_v=cmp_pe_v, cmp_w1_v=cmp_w1_v, cmp_w2_v=cmp_w2_v,
             nsa_w_in=nsa_w_in, nsa_w_o=nsa_w_o, ffn_w_gate=ffn_w_gate, ffn_w_up=ffn_w_up,
             ffn_w_down=ffn_w_down, moe_w_router=moe_w_router, moe_w_gate=moe_w_gate,
             moe_w_up=moe_w_up, moe_w_down=moe_w_down, ple_w_proj=ple_w_proj, ple_w_gate=ple_w_gate)
    p = _prep_weights(w)
    Bp = x_prompt.shape[0]
    gla_zero = jnp.zeros((Bp, GLA_HEADS, GLA_DK, GLA_DV), F32)
    yp, gp, cp, sp, wp = _trunk(x_prompt, p_prompt, 0, gla_zero, None, p)
    past_len = page_table.shape[1] * PAGE_SIZE
    ys, gs, cs, ss, ws = _trunk(x_sample, p_sample, past_len, state_gla[0],
                                (cache_cmp, cache_sel, page_table, state_win), p)
    return (yp, ys, gp, gs, cp, cs, sp, ss, wp, ws)
```

```python
import functools
import math

import jax
import jax.numpy as jnp
from jax import lax
from jax.experimental import pallas as pl
from jax.experimental.pallas import tpu as pltpu

F32 = jnp.float32
BF16 = jnp.bfloat16

D_MODEL = 1024
DEPTH = 2
PAGE_SIZE = 128
PLE_DIM = 256
DEEPNORM_ALPHA = (2.0 * DEPTH) ** 0.25
LN_EPS = 1e-5
ROPE_THETA = 10000.0

GLA_HEADS = 4
GLA_DK = 128
GLA_DV = 256
GLA_GATE_RANK = 16
GLA_TAU = 16.0
GLA_CHUNK = 64

NSA_HEADS = 16
NSA_HD = 64
NSA_GROUPS = 2
NSA_HPG = 8
CMP_STRIDE = 16
CMP_LEN = 32
CMP_HIDDEN = 128
SEL_BLOCK = 64
SEL_TOP = 16
WINDOW = 512
NEG_INF = -1e30
FORCE_SCORE = 1e9

N_EXPERTS = 8
LANES = 128
VMEM_LIMIT = 56 * 1024 * 1024

_NT = (((1,), (1,)), ((), ()))
_TN = (((0,), (0,)), ((), ()))


def _cp(*sem):
    return pltpu.CompilerParams(dimension_semantics=sem, vmem_limit_bytes=VMEM_LIMIT)


def _dot(a, b):
    return jnp.dot(a, b, preferred_element_type=F32)


def _dot_nt(a, b):
    return lax.dot_general(a, b, _NT, preferred_element_type=F32)


def _dot_tn(a, b):
    return lax.dot_general(a, b, _TN, preferred_element_type=F32)


def _split2(x):
    hi = x.astype(BF16)
    lo = (x - hi.astype(F32)).astype(BF16)
    return hi, lo


def _split3(x):
    hi = x.astype(BF16)
    r = x - hi.astype(F32)
    mid = r.astype(BF16)
    lo = (r - mid.astype(F32)).astype(BF16)
    return hi, mid, lo


def _vmod(x, n):
    assert n & (n - 1) == 0
    return x & (n - 1)


def _vdiv(x, n):
    assert n & (n - 1) == 0
    return x >> (n.bit_length() - 1)


def _layer_norm(y, g, b):
    mu = jnp.mean(y, axis=-1, keepdims=True)
    yc = y - mu
    var = jnp.mean(yc * yc, axis=-1, keepdims=True)
    return yc * lax.rsqrt(var + LN_EPS) * g + b


def _rope_lanes(x, cos, sin):
    lane = lax.broadcasted_iota(jnp.int32, x.shape, x.ndim - 1)
    first = _vmod(lane, NSA_HD) < (NSA_HD // 2)
    partner = jnp.where(first, pltpu.roll(x, LANES - NSA_HD // 2, x.ndim - 1),
                        pltpu.roll(x, NSA_HD // 2, x.ndim - 1))
    return x * cos + partner * sin


def _proj_body(x_ref, w_ref, o_ref):
    o_ref[...] = _dot(x_ref[...], w_ref[...]).astype(o_ref.dtype)


def _proj(x, w, tm, tn, out_dtype):
    M, K = x.shape
    N = w.shape[1]
    return pl.pallas_call(
        _proj_body, grid=(M // tm, N // tn),
        in_specs=[pl.BlockSpec((tm, K), lambda i, j: (i, 0)),
                  pl.BlockSpec((K, tn), lambda i, j: (0, j))],
        out_specs=pl.BlockSpec((tm, tn), lambda i, j: (i, j)),
        out_shape=jax.ShapeDtypeStruct((M, N), out_dtype),
        compiler_params=_cp("parallel", "arbitrary"), name="proj")(x, w)


def _gla_gate_body(x_ref, wa_ref, wa2_ref, ba_ref, o_ref):
    a = _dot(x_ref[...], wa_ref[...])
    z = _dot(a.astype(BF16), wa2_ref[...]) + ba_ref[...]
    o_ref[...] = (jnp.minimum(z, 0.0) - jnp.log1p(jnp.exp(-jnp.abs(z)))) * (1.0 / GLA_TAU)


def _gla_gate(xb, wa, wa2, ba, tm):
    M = xb.shape[0]
    N = wa2.shape[1]
    return pl.pallas_call(
        _gla_gate_body, grid=(M // tm,),
        in_specs=[pl.BlockSpec((tm, D_MODEL), lambda i: (i, 0)),
                  pl.BlockSpec((D_MODEL, LANES), lambda i: (0, 0)),
                  pl.BlockSpec((LANES, N), lambda i: (0, 0)),
                  pl.BlockSpec((1, N), lambda i: (0, 0))],
        out_specs=pl.BlockSpec((tm, N), lambda i: (i, 0)),
        out_shape=jax.ShapeDtypeStruct((M, N), F32),
        compiler_params=_cp("parallel"), name="gla_gate")(xb, wa, wa2, ba)


def _gla_body(q_ref, k_ref, v_ref, r_ref, la_ref, g_ref, s0_ref, o_ref, sout_ref, s_ref, *, C, NC):
    t = pl.program_id(1)

    @pl.when(t == 0)
    def _():
        s_ref[...] = s0_ref[0]

    row = lax.broadcasted_iota(jnp.int32, (C, C), 0)
    col = lax.broadcasted_iota(jnp.int32, (C, C), 1)
    tri = col <= row
    tri_b = tri.astype(BF16)
    ones = jnp.ones((C, LANES), BF16)
    rowk = lax.broadcasted_iota(jnp.int32, (C, GLA_DK), 0)
    for h in range(GLA_HEADS):
        kc = slice(h * GLA_DK, (h + 1) * GLA_DK)
        vc = slice(h * GLA_DV, (h + 1) * GLA_DV)
        state = s_ref[h]
        for c in range(NC):
            sl = pl.ds(c * C, C)
            la = la_ref[sl, kc]
            hi, mid, lo = _split3(la)
            b = _dot(tri_b, hi) + _dot(tri_b, mid) + _dot(tri_b, lo)
            tot = _dot_tn(hi, ones) + _dot_tn(mid, ones) + _dot_tn(lo, ones)
            b_mid = jnp.sum(jnp.where(rowk == (C - 1) // 2, b, 0.0), axis=0, keepdims=True)
            b_last = jnp.sum(jnp.where(rowk == C - 1, b, 0.0), axis=0, keepdims=True)
            q = q_ref[sl, kc] * (GLA_DK ** -0.5)
            k = k_ref[sl, kc]
            v = v_ref[sl, vc].astype(BF16)
            qe = (q * jnp.exp(b - b_mid)).astype(BF16)
            ke = (k * jnp.exp(b_mid - b)).astype(BF16)
            a = jnp.where(tri, _dot_nt(qe, ke), 0.0)
            o = _dot(a.astype(BF16), v) + _dot((q * jnp.exp(b)).astype(BF16), state.astype(BF16))
            kd = (k * jnp.exp(b_last - b)).astype(BF16)
            state = jnp.exp(tot[:, 0:1]) * state + _dot_tn(kd, v)
            ms = jnp.mean(o * o, axis=-1, keepdims=True)
            rr = r_ref[sl, vc]
            o = o * lax.rsqrt(ms + LN_EPS) * g_ref[...] * (rr * jax.nn.sigmoid(rr))
            o_ref[sl, vc] = o.astype(o_ref.dtype)
        s_ref[h] = state

    @pl.when(t == pl.num_programs(1) - 1)
    def _():
        sout_ref[0] = s_ref[...]


def _gla(proj, la, norm_g, s0, B, T, C, NC):
    Tt = C * NC
    nT = T // Tt
    M = B * T
    hk, hv = GLA_HEADS * GLA_DK, GLA_HEADS * GLA_DV
    body = functools.partial(_gla_body, C=C, NC=NC)
    state_spec = pl.BlockSpec((1, GLA_HEADS, GLA_DK, GLA_DV), lambda b, t: (b, 0, 0, 0))
    return pl.pallas_call(
        body, grid=(B, nT),
        in_specs=[pl.BlockSpec((Tt, hk), lambda b, t: (b * nT + t, 0)),
                  pl.BlockSpec((Tt, hk), lambda b, t: (b * nT + t, 1)),
                  pl.BlockSpec((Tt, hv), lambda b, t: (b * nT + t, 2 * hk // hv)),
                  pl.BlockSpec((Tt, hv), lambda b, t: (b * nT + t, 2 * hk // hv + 1)),
                  pl.BlockSpec((Tt, hk), lambda b, t: (b * nT + t, 0)),
                  pl.BlockSpec((1, GLA_DV), lambda b, t: (0, 0)),
                  state_spec],
        out_specs=[pl.BlockSpec((Tt, hv), lambda b, t: (b * nT + t, 0)), state_spec],
        out_shape=[jax.ShapeDtypeStruct((M, hv), BF16),
                   jax.ShapeDtypeStruct((B, GLA_HEADS, GLA_DK, GLA_DV), F32)],
        scratch_shapes=[pltpu.VMEM((GLA_HEADS, GLA_DK, GLA_DV), F32)],
        compiler_params=_cp("parallel", "arbitrary"), name="gla")(
            proj, proj, proj, proj, la, norm_g, s0)


def _mm_ln_body(a_ref, w_ref, r_ref, g_ref, b_ref, of_ref, ob_ref):
    y = DEEPNORM_ALPHA * r_ref[...] + _dot(a_ref[...], w_ref[...])
    out = _layer_norm(y, g_ref[...], b_ref[...])
    of_ref[...] = out
    ob_ref[...] = out.astype(BF16)


def _mm_ln(a, w, resid, g, b, tm):
    M, K = a.shape
    N = w.shape[1]
    return pl.pallas_call(
        _mm_ln_body, grid=(M // tm,),
        in_specs=[pl.BlockSpec((tm, K), lambda i: (i, 0)),
                  pl.BlockSpec((K, N), lambda i: (0, 0)),
                  pl.BlockSpec((tm, N), lambda i: (i, 0)),
                  pl.BlockSpec((1, N), lambda i: (0, 0)),
                  pl.BlockSpec((1, N), lambda i: (0, 0))],
        out_specs=[pl.BlockSpec((tm, N), lambda i: (i, 0)),
                   pl.BlockSpec((tm, N), lambda i: (i, 0))],
        out_shape=[jax.ShapeDtypeStruct((M, N), F32), jax.ShapeDtypeStruct((M, N), BF16)],
        compiler_params=_cp("parallel"), name="mm_ln")(a, w, resid, g, b)


def _ffn_body(xb_ref, xf_ref, wg_ref, wu_ref, wd_ref, g_ref, b_ref, of_ref, ob_ref, acc_ref):
    j = pl.program_id(1)

    @pl.when(j == 0)
    def _():
        acc_ref[...] = jnp.zeros_like(acc_ref)

    x = xb_ref[...]
    hg = _dot(x, wg_ref[...])
    hu = _dot(x, wu_ref[...])
    h = (hg * jax.nn.sigmoid(hg)) * hu
    acc_ref[...] += _dot(h.astype(BF16), wd_ref[...])

    @pl.when(j == pl.num_programs(1) - 1)
    def _():
        out = _layer_norm(DEEPNORM_ALPHA * xf_ref[...] + acc_ref[...], g_ref[...], b_ref[...])
        of_ref[...] = out
        ob_ref[...] = out.astype(BF16)


def _ffn(xb, xf, wg, wu, wd, g, b, tm, tf):
    M = xb.shape[0]
    F = wg.shape[1]
    return pl.pallas_call(
        _ffn_body, grid=(M // tm, F // tf),
        in_specs=[pl.BlockSpec((tm, D_MODEL), lambda i, j: (i, 0)),
                  pl.BlockSpec((tm, D_MODEL), lambda i, j: (i, 0)),
                  pl.BlockSpec((D_MODEL, tf), lambda i, j: (0, j)),
                  pl.BlockSpec((D_MODEL, tf), lambda i, j: (0, j)),
                  pl.BlockSpec((tf, D_MODEL), lambda i, j: (j, 0)),
                  pl.BlockSpec((1, D_MODEL), lambda i, j: (0, 0)),
                  pl.BlockSpec((1, D_MODEL), lambda i, j: (0, 0))],
        out_specs=[pl.BlockSpec((tm, D_MODEL), lambda i, j: (i, 0)),
                   pl.BlockSpec((tm, D_MODEL), lambda i, j: (i, 0))],
        out_shape=[jax.ShapeDtypeStruct((M, D_MODEL), F32), jax.ShapeDtypeStruct((M, D_MODEL), BF16)],
        scratch_shapes=[pltpu.VMEM((tm, D_MODEL), F32)],
        compiler_params=_cp("parallel", "arbitrary"), name="ffn")(xb, xf, wg, wu, wd, g, b)


def _ple_body(xb_ref, xf_ref, pb_ref, wg_ref, wp_ref, of_ref, ob_ref):
    gate = jax.nn.sigmoid(_dot(xb_ref[...], wg_ref[...]))
    out = xf_ref[...] + gate * _dot(pb_ref[...], wp_ref[...])
    of_ref[...] = out
    ob_ref[...] = out.astype(BF16)


def _ple(xb, xf, pb, wg, wp, tm):
    M = xb.shape[0]
    return pl.pallas_call(
        _ple_body, grid=(M // tm,),
        in_specs=[pl.BlockSpec((tm, D_MODEL), lambda i: (i, 0)),
                  pl.BlockSpec((tm, D_MODEL), lambda i: (i, 0)),
                  pl.BlockSpec((tm, PLE_DIM), lambda i: (i, 0)),
                  pl.BlockSpec((D_MODEL, D_MODEL), lambda i: (0, 0)),
                  pl.BlockSpec((PLE_DIM, D_MODEL), lambda i: (0, 0))],
        out_specs=[pl.BlockSpec((tm, D_MODEL), lambda i: (i, 0)),
                   pl.BlockSpec((tm, D_MODEL), lambda i: (i, 0))],
        out_shape=[jax.ShapeDtypeStruct((M, D_MODEL), F32), jax.ShapeDtypeStruct((M, D_MODEL), BF16)],
        compiler_params=_cp("parallel"), name="ple")(xb, xf, pb, wg, wp)


def _router_body(x_ref, w_ref, info_ref, cnt_ref):
    x = x_ref[...]
    xh, xl = _split2(x)
    w = w_ref[...]
    wh, wl = _split2(w)
    logits = _dot(xh, wh) + _dot(xl, wh) + _dot(xh, wl)
    lane = lax.broadcasted_iota(jnp.int32, logits.shape, 1)
    live = lane < N_EXPERTS
    lg = jnp.where(live, logits, -jnp.inf)
    m1 = jnp.max(lg, axis=-1, keepdims=True)
    i1 = jnp.min(jnp.where(lg == m1, lane, LANES), axis=-1, keepdims=True)
    lg2 = jnp.where(lane == i1, -jnp.inf, lg)
    m2 = jnp.max(lg2, axis=-1, keepdims=True)
    i2 = jnp.min(jnp.where(lg2 == m2, lane, LANES), axis=-1, keepdims=True)
    e2 = jnp.exp(m2 - m1)
    den = 1.0 + e2
    onehot = jnp.where((lane == i1) | (lane == i2), 1.0, 0.0)
    tb = x.shape[0]
    before = (lax.broadcasted_iota(jnp.int32, (tb, tb), 1) < lax.broadcasted_iota(jnp.int32, (tb, tb), 0))
    cum = _dot(before.astype(BF16), onehot.astype(BF16))
    r1 = jnp.sum(jnp.where(lane == i1, cum, 0.0), axis=-1, keepdims=True)
    r2 = jnp.sum(jnp.where(lane == i2, cum, 0.0), axis=-1, keepdims=True)
    cols = (i1.astype(F32), i2.astype(F32), 1.0 / den, e2 / den, r1, r2)
    info = jnp.zeros(logits.shape, F32)
    for c, val in enumerate(cols):
        info = jnp.where(lane == c, val, info)
    info_ref[...] = info
    cnt_ref[0] = jnp.broadcast_to(jnp.sum(onehot, axis=0, keepdims=True), (8, LANES))


def _router(xf, w_pad, tb):
    M = xf.shape[0]
    return pl.pallas_call(
        _router_body, grid=(M // tb,),
        in_specs=[pl.BlockSpec((tb, D_MODEL), lambda i: (i, 0)),
                  pl.BlockSpec((D_MODEL, LANES), lambda i: (0, 0))],
        out_specs=[pl.BlockSpec((tb, LANES), lambda i: (i, 0)),
                   pl.BlockSpec((1, 8, LANES), lambda i: (i, 0, 0))],
        out_shape=[jax.ShapeDtypeStruct((M, LANES), F32),
                   jax.ShapeDtypeStruct((M // tb, 8, LANES), F32)],
        compiler_params=_cp("parallel"), name="router")(xf, w_pad)


MOE_TILE = 128
MOE_GROUP = 4


def _moe_sizes(M, tb):
    nb = M // tb
    LT = 2 * tb // MOE_TILE + N_EXPERTS
    n_tiles = -(-(nb * LT) // MOE_GROUP) * MOE_GROUP + N_EXPERTS * MOE_GROUP
    return nb, LT, n_tiles, n_tiles // MOE_GROUP


def _moe_plan(counts, LT, n_tiles, n_groups):
    nt = (counts + MOE_TILE - 1) // MOE_TILE
    lend = jnp.cumsum(nt, axis=1)
    lstart = lend - nt
    lt = jnp.arange(LT, dtype=jnp.int32)
    used = lt[None, :] < lend[:, -1:]
    e_of = jnp.minimum(jnp.sum(lt[None, :, None] >= lend[:, None, :], axis=-1), N_EXPERTS - 1).astype(jnp.int32)
    k_in = lt[None, :] - jnp.take_along_axis(lstart, e_of, axis=1)
    tot = jnp.sum(nt, axis=0)
    tot_pad = (tot + MOE_GROUP - 1) // MOE_GROUP * MOE_GROUP
    gend = jnp.cumsum(tot_pad)
    goff = gend - tot_pad
    boff = jnp.cumsum(nt, axis=0) - nt
    dt = goff[e_of] + jnp.take_along_axis(boff, e_of, axis=1) + k_in
    g4 = jnp.arange(n_groups, dtype=jnp.int32) * MOE_GROUP
    nu = (gend[-1] // MOE_GROUP).astype(jnp.int32)
    g4 = jnp.minimum(g4, gend[-1] - MOE_GROUP)
    ge = jnp.sum(g4[:, None] >= gend[None, :], axis=-1).astype(jnp.int32)
    return dict(le=jnp.where(used, e_of, -1).astype(jnp.int32),
                lr=(k_in * MOE_TILE).astype(jnp.int32),
                dt_out=jnp.where(used, dt, n_tiles).astype(jnp.int32),
                dt_in=jnp.where(used, dt, 0).astype(jnp.int32),
                ge=ge, nu=nu.reshape(1))


def _moe_slot_match(info, e, r0):
    ef = e.astype(F32)
    slot = (r0 + lax.broadcasted_iota(jnp.int32, (1, MOE_TILE), 1)).astype(F32)
    m1 = (info[:, 0:1] == ef) & (info[:, 4:5] == slot)
    m2 = (info[:, 1:2] == ef) & (info[:, 5:6] == slot)
    return m1, m2


def _moe_dispatch_body(le_ref, lr_ref, dt_ref, x_ref, info_ref, init_ref, o_ref):
    b = pl.program_id(0)
    lt = pl.program_id(1)
    m1, m2 = _moe_slot_match(info_ref[...], le_ref[b, lt], lr_ref[b, lt])
    sel = jnp.where(m1 | m2, 1.0, 0.0).astype(BF16)
    o_ref[...] = _dot_tn(sel, x_ref[...]).astype(BF16)


def _moe_dispatch(xb, info, plan, tb, n_tiles):
    M = xb.shape[0]
    nb, LT = plan["le"].shape
    zeros = jnp.zeros(((n_tiles + 1) * MOE_TILE, D_MODEL), BF16)
    grid_spec = pltpu.PrefetchScalarGridSpec(
        num_scalar_prefetch=3, grid=(nb, LT),
        in_specs=[pl.BlockSpec((tb, D_MODEL), lambda b, lt, le, lr, dt: (b, 0)),
                  pl.BlockSpec((tb, LANES), lambda b, lt, le, lr, dt: (b, 0)),
                  pl.BlockSpec(memory_space=pl.ANY)],
        out_specs=pl.BlockSpec((MOE_TILE, D_MODEL), lambda b, lt, le, lr, dt: (dt[b, lt], 0)))
    return pl.pallas_call(
        _moe_dispatch_body, grid_spec=grid_spec,
        out_shape=jax.ShapeDtypeStruct(zeros.shape, BF16),
        input_output_aliases={5: 0},
        compiler_params=_cp("arbitrary", "arbitrary"), name="moe_dispatch")(
            plan["le"], plan["lr"], plan["dt_out"], xb, info, zeros)


def _moe_ffn_body(ge_ref, nu_ref, x_ref, wg_ref, wu_ref, wd_ref, o_ref, acc_ref):
    j = pl.program_id(1)

    @pl.when(pl.program_id(0) < nu_ref[0])
    def _():
        @pl.when(j == 0)
        def _():
            acc_ref[...] = jnp.zeros_like(acc_ref)

        x = x_ref[...]
        hg = _dot(x, wg_ref[...])
        hu = _dot(x, wu_ref[...])
        h = (hg * jax.nn.sigmoid(hg)) * hu
        acc_ref[...] += _dot(h.astype(BF16), wd_ref[...])

        @pl.when(j == pl.num_programs(1) - 1)
        def _():
            o_ref[...] = acc_ref[...].astype(BF16)


def _moe_ffn(xs, plan, wg, wu, wd, n_groups, tf):
    tm = MOE_TILE * MOE_GROUP
    F = wg.shape[2]
    nf = F // tf
    row = lambda g, j, ge, nu: (jnp.minimum(g, nu[0] - 1), 0)
    col = lambda g, j, nu: jnp.where(g < nu[0], j, nf - 1)
    grid_spec = pltpu.PrefetchScalarGridSpec(
        num_scalar_prefetch=2, grid=(n_groups, nf),
        in_specs=[pl.BlockSpec((tm, D_MODEL), row),
                  pl.BlockSpec((None, D_MODEL, tf), lambda g, j, ge, nu: (ge[g], 0, col(g, j, nu))),
                  pl.BlockSpec((None, D_MODEL, tf), lambda g, j, ge, nu: (ge[g], 0, col(g, j, nu))),
                  pl.BlockSpec((None, tf, D_MODEL), lambda g, j, ge, nu: (ge[g], col(g, j, nu), 0))],
        out_specs=pl.BlockSpec((tm, D_MODEL), row),
        scratch_shapes=[pltpu.VMEM((tm, D_MODEL), F32)])
    return pl.pallas_call(
        _moe_ffn_body, grid_spec=grid_spec,
        out_shape=jax.ShapeDtypeStruct((n_groups * tm, D_MODEL), BF16),
        compiler_params=_cp("arbitrary", "arbitrary"), name="moe_ffn")(
            plan["ge"], plan["nu"], xs, wg, wu, wd)


def _moe_combine_body(le_ref, lr_ref, dt_ref, y_ref, info_ref, xf_ref, g_ref, b_ref, of_ref, ob_ref, acc_ref):
    b = pl.program_id(0)
    lt = pl.program_id(1)
    e = le_ref[b, lt]

    @pl.when(lt == 0)
    def _():
        acc_ref[...] = jnp.zeros_like(acc_ref)

    @pl.when(e >= 0)
    def _():
        info = info_ref[...]
        m1, m2 = _moe_slot_match(info, e, lr_ref[b, lt])
        pw = (jnp.where(m1, info[:, 2:3], 0.0) + jnp.where(m2, info[:, 3:4], 0.0)).astype(BF16)
        acc_ref[...] += _dot(pw, y_ref[...])

    @pl.when(lt == pl.num_programs(1) - 1)
    def _():
        out = _layer_norm(DEEPNORM_ALPHA * xf_ref[...] + acc_ref[...], g_ref[...], b_ref[...])
        of_ref[...] = out
        ob_ref[...] = out.astype(BF16)


def _moe_combine(ys, info, xf, plan, g, b, tb):
    M = xf.shape[0]
    nb, LT = plan["le"].shape
    blk = lambda bb, lt, le, lr, dt: (bb, 0)
    const = lambda bb, lt, le, lr, dt: (0, 0)
    grid_spec = pltpu.PrefetchScalarGridSpec(
        num_scalar_prefetch=3, grid=(nb, LT),
        in_specs=[pl.BlockSpec((MOE_TILE, D_MODEL), lambda bb, lt, le, lr, dt: (dt[bb, lt], 0)),
                  pl.BlockSpec((tb, LANES), blk),
                  pl.BlockSpec((tb, D_MODEL), blk),
                  pl.BlockSpec((1, D_MODEL), const),
                  pl.BlockSpec((1, D_MODEL), const)],
        out_specs=[pl.BlockSpec((tb, D_MODEL), blk), pl.BlockSpec((tb, D_MODEL), blk)],
        scratch_shapes=[pltpu.VMEM((tb, D_MODEL), F32)])
    return pl.pallas_call(
        _moe_combine_body, grid_spec=grid_spec,
        out_shape=[jax.ShapeDtypeStruct((M, D_MODEL), F32), jax.ShapeDtypeStruct((M, D_MODEL), BF16)],
        compiler_params=_cp("arbitrary", "arbitrary"), name="moe_combine")(
            plan["le"], plan["lr"], plan["dt_in"], ys, info, xf, g, b)


def _nsa_kv_body(s_ref, w_ref, cos_ref, sin_ref, cmp_ref, *rest, tm, nT, with_kvx):
    rows = _dot(s_ref[...], w_ref[...])
    cos = cos_ref[...]
    sin = sin_ref[...]
    cmp_ref[...] = rows[:, 0:256]
    ks = _rope_lanes(rows[:, 256:384], cos, sin)
    vs = rows[:, 384:512]
    kw = _rope_lanes(rows[:, 512:640], cos, sin)
    vw = rows[:, 640:768]
    if not with_kvx:
        sel_ref, win_ref = rest
        sel_ref[:, 0:128] = ks
        sel_ref[:, 128:256] = vs
        win_ref[:, 0:128] = kw
        win_ref[:, 128:256] = vw
    else:
        kvx_ref, cmp_t_ref, sel_t_ref, win_t_ref = rest
        cmp_t_ref[0, 0] = rows[:, 0:128].T
        cmp_t_ref[0, 1] = rows[:, 128:256].T
        sel_t_ref[0, 0] = ks.T
        sel_t_ref[0, 1] = vs.T
        win_t_ref[0, 0] = kw.T
        win_t_ref[0, 1] = vw.T
        lane = lax.broadcasted_iota(jnp.int32, (tm, LANES), 1)
        pos = (pl.program_id(0) % nT) * tm + lax.broadcasted_iota(jnp.int32, (tm, LANES), 0)
        low = lane < NSA_HD
        onehot = jnp.where(lane - NSA_HD == _vdiv(pos, SEL_BLOCK), 1.0, 0.0)
        for g in range(NSA_GROUPS):
            pick = (lambda x: x) if g == 0 else (lambda x: pltpu.roll(x, NSA_HD, 1))
            kvx_ref[g, :, 0:128] = jnp.where(low, pick(ks), onehot).astype(BF16)
            kvx_ref[g, :, 128:256] = jnp.where(low, pick(vs), 0.0).astype(BF16)
            kvx_ref[g, :, 256:384] = jnp.where(low, pick(kw), 0.0).astype(BF16)
            kvx_ref[g, :, 384:512] = jnp.where(low, pick(vw), 0.0).astype(BF16)


def _nsa_kv(sb, w, cos, sin, tm, with_kvx):
    M = sb.shape[0]
    nT = cos.shape[0] // tm
    body = functools.partial(_nsa_kv_body, tm=tm, nT=nT, with_kvx=with_kvx)
    if with_kvx:
        T = cos.shape[0]
        t_spec = pl.BlockSpec((1, 2, LANES, tm), lambda i: (i // nT, 0, 0, i % nT))
        t_shape = jax.ShapeDtypeStruct((M // T, 2, LANES, T), F32)
        out_specs = [pl.BlockSpec((tm, 256), lambda i: (i, 0)),
                     pl.BlockSpec((NSA_GROUPS, tm, 512), lambda i: (0, i, 0)), t_spec, t_spec, t_spec]
        out_shape = [jax.ShapeDtypeStruct((M, 256), F32),
                     jax.ShapeDtypeStruct((NSA_GROUPS, M, 512), BF16), t_shape, t_shape, t_shape]
    else:
        out_specs = [pl.BlockSpec((tm, 256), lambda i: (i, 0))] * 3
        out_shape = [jax.ShapeDtypeStruct((M, 256), F32)] * 3
    return pl.pallas_call(
        body, grid=(M // tm,),
        in_specs=[pl.BlockSpec((tm, D_MODEL), lambda i: (i, 0)),
                  pl.BlockSpec((D_MODEL, 768), lambda i: (0, 0)),
                  pl.BlockSpec((tm, LANES), lambda i: (i % nT, 0)),
                  pl.BlockSpec((tm, LANES), lambda i: (i % nT, 0))],
        out_specs=out_specs, out_shape=out_shape,
        compiler_params=_cp("parallel"), name="nsa_kv")(sb, w, cos, sin)


def _nsa_q_body(x_ref, w_ref, cos_ref, sin_ref, o_ref):
    res = _dot(x_ref[...], w_ref[...])
    cos = cos_ref[...]
    sin = sin_ref[...]
    for h in range(NSA_HEADS):
        xh = res[:, h * LANES:(h + 1) * LANES]
        o_ref[h] = (_rope_lanes(xh, cos, sin) * (NSA_HD ** -0.5)).astype(o_ref.dtype)


def _nsa_q(xb, w_pad, cos, sin, tm, out_dtype):
    M = xb.shape[0]
    nT = cos.shape[0] // tm
    return pl.pallas_call(
        _nsa_q_body, grid=(M // tm,),
        in_specs=[pl.BlockSpec((tm, D_MODEL), lambda i: (i, 0)),
                  pl.BlockSpec((D_MODEL, NSA_HEADS * LANES), lambda i: (0, 0)),
                  pl.BlockSpec((tm, LANES), lambda i: (i % nT, 0)),
                  pl.BlockSpec((tm, LANES), lambda i: (i % nT, 0))],
        out_specs=pl.BlockSpec((NSA_HEADS, tm, LANES), lambda i: (0, i, 0)),
        out_shape=jax.ShapeDtypeStruct((NSA_HEADS, M, LANES), out_dtype),
        compiler_params=_cp("parallel"), name="nsa_q")(xb, w_pad, cos, sin)


def _nsa_gate_body(x_ref, w_ref, o_ref):
    o_ref[0] = jax.nn.sigmoid(_dot(x_ref[...], w_ref[0]))


def _nsa_gate(xb, w_pad, tm):
    M = xb.shape[0]
    S = w_pad.shape[0]
    return pl.pallas_call(
        _nsa_gate_body, grid=(M // tm, S),
        in_specs=[pl.BlockSpec((tm, D_MODEL), lambda i, s: (i, 0)),
                  pl.BlockSpec((1, D_MODEL, LANES), lambda i, s: (s, 0, 0))],
        out_specs=pl.BlockSpec((1, tm, LANES), lambda i, s: (s, i, 0)),
        out_shape=jax.ShapeDtypeStruct((S, M, LANES), F32),
        compiler_params=_cp("parallel", "arbitrary"), name="nsa_gate")(xb, w_pad)


def _compress_rows(load, n_rows, w1_refs, w2_refs, pe_refs):
    lane = lax.broadcasted_iota(jnp.int32, (n_rows, LANES), 1)
    low = lane < NSA_HD
    out = []
    for kv in range(2):
        w1 = w1_refs[kv][...]
        bias = (_dot(pe_refs[kv][0], w1)[0:1, 0:CMP_HIDDEN]
                + _dot(pe_refs[kv][1], w1)[0:1, CMP_HIDDEN:2 * CMP_HIDDEN])
        g0, g1 = [], []
        for rp in range(CMP_STRIDE // 2):
            p = load(kv, 2 * rp)
            q = load(kv, 2 * rp + 1)
            g0.append(jnp.where(low, p, pltpu.roll(q, NSA_HD, 1)).astype(BF16))
            g1.append(jnp.where(low, pltpu.roll(p, NSA_HD, 1), q).astype(BF16))
        per_g = []
        for cols in (g0, g1):
            ab = _dot(jnp.concatenate(cols, axis=1), w1)
            nxt = pltpu.roll(ab[:, CMP_HIDDEN:2 * CMP_HIDDEN], n_rows - 1, 0)
            h = ab[:, 0:CMP_HIDDEN] + nxt + bias
            per_g.append(_dot(jax.nn.gelu(h).astype(BF16), w2_refs[kv][...]))
        out.append(per_g)
    return out


def _compress_prompt_body(x_ref, w1k, w1v, w2k, w2v, pek, pev, cos_ref, sin_ref, o_ref, *, n_rows):
    load = lambda kv, r: x_ref[:, r * 256 + kv * LANES:r * 256 + (kv + 1) * LANES]
    c = _compress_rows(load, n_rows, (w1k, w1v), (w2k, w2v), (pek, pev))
    for g in range(NSA_GROUPS):
        o_ref[0, g, 0] = _rope_lanes(c[0][g], cos_ref[...], sin_ref[...]).astype(BF16)
        o_ref[0, g, 1] = c[1][g].astype(BF16)


def _compress_prompt(xflat, cw, cos, sin, B, n_rows):
    body = functools.partial(_compress_prompt_body, n_rows=n_rows)
    const2 = lambda b: (0, 0)
    const3 = lambda b: (0, 0, 0)
    return pl.pallas_call(
        body, grid=(B,),
        in_specs=[pl.BlockSpec((n_rows, 4096), lambda b: (b, 0)),
                  pl.BlockSpec((1024, 256), const2), pl.BlockSpec((1024, 256), const2),
                  pl.BlockSpec((CMP_HIDDEN, LANES), const2), pl.BlockSpec((CMP_HIDDEN, LANES), const2),
                  pl.BlockSpec((2, 16, 1024), const3), pl.BlockSpec((2, 16, 1024), const3),
                  pl.BlockSpec((n_rows, LANES), const2), pl.BlockSpec((n_rows, LANES), const2)],
        out_specs=pl.BlockSpec((1, NSA_GROUPS, 2, n_rows, LANES), lambda b: (b, 0, 0, 0, 0)),
        out_shape=jax.ShapeDtypeStruct((B, NSA_GROUPS, 2, n_rows, LANES), BF16),
        compiler_params=_cp("parallel"), name="compress_prompt")(
            xflat, cw["w1k"], cw["w1v"], cw["w2k"], cw["w2v"], cw["pek"], cw["pev"], cos, sin)


def _compress_sample_body(pt_ref, *refs, n_rows, PG):
    pages = refs[:PG]
    w1k, w1v, w2k, w2v, pek, pev, cos_ref, sin_ref, o_ref, x_ref = refs[PG:]
    j = pl.program_id(1)
    for k in range(PG):
        p0 = pl.multiple_of((j * PG + k) * PAGE_SIZE, PAGE_SIZE)
        for kv in range(2):
            x_ref[kv, pl.ds(p0, PAGE_SIZE), :] = pages[k][0, kv].T

    @pl.when(j == pl.num_programs(1) - 1)
    def _():
        load = lambda kv, r: x_ref[kv, pl.ds(r, n_rows, stride=CMP_STRIDE), :]
        c = _compress_rows(load, n_rows, (w1k, w1v), (w2k, w2v), (pek, pev))
        ck = c[0][0] + pltpu.roll(c[0][1], NSA_HD, 1)
        cv = c[1][0] + pltpu.roll(c[1][1], NSA_HD, 1)
        o_ref[0, :, 0:128] = _rope_lanes(ck, cos_ref[...], sin_ref[...]).astype(BF16)
        o_ref[0, :, 128:256] = cv.astype(BF16)


def _compress_sample(cache_t, page_table, cw, cos, sin, PG):
    B, n_pages = page_table.shape
    n_rows = n_pages * PAGE_SIZE // CMP_STRIDE
    body = functools.partial(_compress_sample_body, n_rows=n_rows, PG=PG)
    const2 = lambda b, j, pt: (0, 0)
    const3 = lambda b, j, pt: (0, 0, 0)
    page_specs = [pl.BlockSpec((1, 2, LANES, PAGE_SIZE), functools.partial(
        lambda b, j, pt, k: (pt[b, j * PG + k], 0, 0, 0), k=k)) for k in range(PG)]
    grid_spec = pltpu.PrefetchScalarGridSpec(
        num_scalar_prefetch=1, grid=(B, n_pages // PG),
        in_specs=page_specs + [
            pl.BlockSpec((1024, 256), const2), pl.BlockSpec((1024, 256), const2),
            pl.BlockSpec((CMP_HIDDEN, LANES), const2), pl.BlockSpec((CMP_HIDDEN, LANES), const2),
            pl.BlockSpec((2, 16, 1024), const3), pl.BlockSpec((2, 16, 1024), const3),
            pl.BlockSpec((n_rows, LANES), const2), pl.BlockSpec((n_rows, LANES), const2)],
        out_specs=pl.BlockSpec((1, n_rows, 256), lambda b, j, pt: (b, 0, 0)),
        scratch_shapes=[pltpu.VMEM((2, n_pages * PAGE_SIZE, LANES), F32)])
    return pl.pallas_call(
        body, grid_spec=grid_spec,
        out_shape=jax.ShapeDtypeStruct((B, n_rows, 256), BF16),
        compiler_params=_cp("parallel", "arbitrary"), name="compress_sample")(
            page_table, *([cache_t] * PG), cw["w1k"], cw["w1v"], cw["w2k_lo"], cw["w2v_lo"],
            cw["pek"], cw["pev"], cos, sin)


def _nsa_attn_body(q_ref, gate_ref, ckv_ref, ovt_ref, kvx_ref, o_ref,
                   s_ref, p_ref, m_ref, l_ref, a_ref, acc_ref, *, Tq, Tk, n_cmp, n_sel):
    q0 = pl.program_id(2) * Tq
    R = NSA_HPG * Tq
    NB = 32
    q3 = q_ref[...]
    qh = q3.reshape(R, LANES)
    tq_lane = q0 + lax.broadcasted_iota(jnp.int32, (1, Tq), 1)
    t_lane = jnp.concatenate([tq_lane] * NSA_HPG, axis=1)

    ck = ckv_ref[0, 0, 0]
    cv = ckv_ref[0, 0, 1]
    n_rows = ck.shape[0]
    s = _dot_nt(ck, qh)
    nidx = lax.broadcasted_iota(jnp.int32, (n_rows, 1), 0)
    cmask = (nidx < n_cmp) & (CMP_STRIDE * nidx + (CMP_LEN - 1) <= t_lane)
    s = jnp.where(cmask, s, NEG_INF)
    e = jnp.where(cmask, jnp.exp(s - jnp.max(s, axis=0, keepdims=True)), 0.0)
    p = e / jnp.maximum(jnp.sum(e, axis=0, keepdims=True), 1e-30)
    o_cmp = _dot_tn(cv, p.astype(BF16))
    psum = p[:, 0:Tq]
    for h in range(1, NSA_HPG):
        psum = psum + p[:, h * Tq:(h + 1) * Tq]
    ph, plo = _split2(psum)
    imp = _dot(ovt_ref[...], ph) + _dot(ovt_ref[...], plo)

    sc = imp[NSA_HD:NSA_HD + NB, :]
    jj = lax.broadcasted_iota(jnp.int32, (NB, 1), 0)
    cur = _vdiv(tq_lane, SEL_BLOCK)
    forced = (jj == 0) | (jj == cur) | (jj == cur - 1)
    valid = SEL_BLOCK * jj <= tq_lane
    sc = jnp.where(valid, jnp.where(forced, FORCE_SCORE, sc), -FORCE_SCORE)
    rank = jnp.zeros((NB, Tq), F32)
    for jp in range(n_sel):
        c = sc[jp:jp + 1, :]
        beats = (c > sc) | ((c == sc) & (jp < jj))
        rank = rank + jnp.where(beats, 1.0, 0.0)
    drop = (rank >= float(min(SEL_TOP, n_sel))) & (jj < n_sel)
    bias_t = jnp.concatenate([jnp.zeros((NSA_HD, Tq), F32), jnp.where(drop, NEG_INF, 0.0),
                              jnp.zeros((LANES - NSA_HD - NB, Tq), F32)], axis=0)
    qa = (q3.astype(F32) + bias_t.T[None]).astype(BF16).reshape(R, LANES)

    def reset():
        m_ref[...] = jnp.full(m_ref.shape, NEG_INF, F32)
        l_ref[...] = jnp.zeros(l_ref.shape, F32)
        acc_ref[...] = jnp.zeros(acc_ref.shape, F32)

    def flash_tile(kt, qv, kcol, masked, window):
        k0 = pl.multiple_of(kt * Tk, Tk)
        ka = kvx_ref[0, pl.ds(k0, Tk), kcol:kcol + LANES]
        va = kvx_ref[0, pl.ds(k0, Tk), kcol + LANES:kcol + 2 * LANES]
        s_ref[...] = _dot_nt(ka, qv)
        if masked:
            kpos = k0 + lax.broadcasted_iota(jnp.int32, (Tk, 1), 0)
            ok = kpos <= tq_lane
            if window:
                ok = ok & (kpos > tq_lane - WINDOW)
        for h in range(NSA_HPG):
            cols = slice(h * Tq, (h + 1) * Tq)
            s = s_ref[:, cols]
            if masked:
                s = jnp.where(ok, s, NEG_INF)
            m_old = m_ref[0:1, cols]
            m_new = jnp.maximum(m_old, jnp.max(s, axis=0, keepdims=True))
            a = jnp.exp(m_old - m_new)
            p = jnp.exp(s - m_new)
            l_ref[0:1, cols] = a * l_ref[0:1, cols] + jnp.sum(p, axis=0, keepdims=True)
            m_ref[0:1, cols] = m_new
            a_ref[0:1, cols] = a
            p_ref[:, cols] = p.astype(BF16)
        acc_ref[...] = a_ref[0:1, :] * acc_ref[...] + _dot_tn(va, p_ref[...])

    def run(lo, hi, qv, kcol, masked, window):
        def step(kt, carry):
            flash_tile(kt, qv, kcol, masked, window)
            return carry
        lax.fori_loop(lo, hi, step, 0)

    kd = q0 // Tk
    reset()
    run(0, kd, qa, 0, False, False)
    flash_tile(kd, qa, 0, True, False)
    o_sel = acc_ref[...] / l_ref[0:1, :]

    reset()
    run(jnp.maximum(q0 - WINDOW + 1, 0) // Tk, kd + 1, qh, 2 * LANES, True, True)
    o_win = acc_ref[...] / l_ref[0:1, :]

    gates_t = gate_ref[0].T
    heads = []
    for h in range(NSA_HPG):
        cols = slice(h * Tq, (h + 1) * Tq)
        o_t = (gates_t[h:h + 1, :] * o_cmp[:, cols]
               + gates_t[NSA_HPG + h:NSA_HPG + h + 1, :] * o_sel[:, cols]
               + gates_t[2 * NSA_HPG + h:2 * NSA_HPG + h + 1, :] * o_win[:, cols])
        heads.append(o_t.T)
    low = lax.broadcasted_iota(jnp.int32, (1, LANES), 1) < NSA_HD
    for hp in range(NSA_HPG // 2):
        pair = jnp.where(low, heads[2 * hp], pltpu.roll(heads[2 * hp + 1], NSA_HD, 1))
        o_ref[:, hp * LANES:(hp + 1) * LANES] = pair.astype(o_ref.dtype)


def _nsa_attn_prompt(q, gates, ckv, ov, kvx, B, T, Tq, Tk, n_cmp, n_sel):
    nq = T // Tq
    M = B * T
    n_rows = ckv.shape[3]
    assert Tk % Tq == 0 and Tq % LANES == 0 and n_sel <= 32
    R = NSA_HPG * Tq
    body = functools.partial(_nsa_attn_body, Tq=Tq, Tk=Tk, n_cmp=n_cmp, n_sel=n_sel)
    scratch = [pltpu.VMEM((Tk, R), F32), pltpu.VMEM((Tk, R), BF16), pltpu.VMEM((8, R), F32),
               pltpu.VMEM((8, R), F32), pltpu.VMEM((8, R), F32), pltpu.VMEM((LANES, R), F32)]
    return pl.pallas_call(
        body, grid=(B, NSA_GROUPS, nq),
        in_specs=[pl.BlockSpec((NSA_HPG, Tq, LANES), lambda b, g, i: (g, b * nq + i, 0)),
                  pl.BlockSpec((1, Tq, LANES), lambda b, g, i: (g, b * nq + i, 0)),
                  pl.BlockSpec((1, 1, 2, n_rows, LANES), lambda b, g, i: (b, g, 0, 0, 0)),
                  pl.BlockSpec((LANES, n_rows), lambda b, g, i: (0, 0)),
                  pl.BlockSpec((1, T, 512), lambda b, g, i: (g, b, 0))],
        out_specs=pl.BlockSpec((Tq, NSA_HPG * NSA_HD), lambda b, g, i: (b * nq + i, g)),
        out_shape=jax.ShapeDtypeStruct((M, NSA_HEADS * NSA_HD), BF16),
        scratch_shapes=scratch,
        compiler_params=_cp("parallel", "parallel", "arbitrary"), name="nsa_attn_prompt")(
            q, gates, ckv, ov, kvx)


def _col_softmax_update(s, m, l, acc, v):
    m_new = jnp.maximum(m, jnp.max(s, axis=0, keepdims=True))
    a = jnp.exp(m - m_new)
    p = jnp.exp(s - m_new)
    l = a * l + jnp.sum(p, axis=0, keepdims=True)
    return m_new, l, a, _dot_tn(p.astype(BF16), v)


def _nsa_attn_sample_body(pt_ref, *refs, T, PG, past_len, n_cmp, n_sel, n_rows, wb):
    pages = refs[:PG]
    (q_ref, gate_ref, ckv_ref, ovt_ref, swin_ref, wnew_ref, snew_ref, o_ref,
     qb_ref, selb_ref, m_ref, l_ref, acc_ref, oc_ref, ow_ref) = refs[PG:]
    j = pl.program_id(1)
    NQ = NSA_HEADS * T
    lane = lax.broadcasted_iota(jnp.int32, (1, NQ), 1)
    tl = past_len + _vmod(lane, T)
    eye = (lax.broadcasted_iota(jnp.int32, (NQ, NQ), 0) == lax.broadcasted_iota(jnp.int32, (NQ, NQ), 1))

    def to_rows(x):
        return jnp.sum(jnp.where(eye, x, 0.0), axis=1, keepdims=True)

    @pl.when(j == 0)
    def _():
        qp = q_ref[...].reshape(NQ, LANES)
        rowi = lax.broadcasted_iota(jnp.int32, (NQ, LANES), 0)
        qb = jnp.where(rowi < NQ // 2, qp, pltpu.roll(qp, NSA_HD, 1)).astype(BF16)
        qb_ref[...] = qb

        ck = ckv_ref[0, :, 0:128]
        cv = ckv_ref[0, :, 128:256]
        s = _dot_nt(ck, qb)
        nidx = lax.broadcasted_iota(jnp.int32, (n_rows, 1), 0)
        cmask = (nidx < n_cmp) & (CMP_STRIDE * nidx + (CMP_LEN - 1) <= tl)
        s = jnp.where(cmask, s, NEG_INF)
        e = jnp.where(cmask, jnp.exp(s - jnp.max(s, axis=0, keepdims=True)), 0.0)
        p = e / jnp.maximum(jnp.sum(e, axis=0, keepdims=True), 1e-30)
        oc_ref[...] = _dot_tn(p.astype(BF16), cv)
        ri = lax.broadcasted_iota(jnp.int32, (NQ, NQ), 0)
        ci = lax.broadcasted_iota(jnp.int32, (NQ, NQ), 1)
        same = ((_vdiv(ri, NQ // 2) == _vdiv(ci, NQ // 2)) & (_vmod(ri, T) == _vmod(ci, T))).astype(BF16)
        ph, plo = _split2(p)
        psum = _dot(ph, same) + _dot(plo, same)
        sh, slo = _split2(psum)
        imp = _dot(ovt_ref[...], sh) + _dot(ovt_ref[...], slo)

        nb = imp.shape[0]
        jj = lax.broadcasted_iota(jnp.int32, (nb, 1), 0)
        cur = _vdiv(tl, SEL_BLOCK)
        forced = (jj == 0) | (jj == cur) | (jj == cur - 1)
        valid = SEL_BLOCK * jj <= tl
        score = jnp.where(valid, jnp.where(forced, FORCE_SCORE, imp), -FORCE_SCORE)
        work = jnp.where(jj < n_sel, score, -2e38)
        keep = jnp.zeros(work.shape, jnp.bool_)
        for _ in range(min(SEL_TOP, n_sel)):
            mx = jnp.max(work, axis=0, keepdims=True)
            first = jnp.min(jnp.where(work == mx, jj, nb), axis=0, keepdims=True)
            pick = jj == first
            keep = keep | pick
            work = jnp.where(pick, -3e38, work)
        selb_ref[...] = jnp.where(keep, 0.0, NEG_INF)

        kw = swin_ref[0, 0].T.astype(BF16)
        vw = swin_ref[0, 1].T.astype(BF16)
        sw = _dot_nt(kw, qb)
        kpos = past_len - wb + lax.broadcasted_iota(jnp.int32, (wb, 1), 0)
        sw = jnp.where((kpos <= tl) & (kpos > tl - WINDOW), sw, NEG_INF)
        zero = jnp.zeros((T, 256), F32)
        wn = jnp.concatenate([wnew_ref[...], zero], axis=0)
        sn = _dot_nt(wn[:, 0:128].astype(BF16), qb)
        kposn = past_len + lax.broadcasted_iota(jnp.int32, (2 * T, 1), 0)
        sn = jnp.where((kposn <= tl) & (kposn > tl - WINDOW), sn, NEG_INF)
        mw = jnp.maximum(jnp.max(sw, axis=0, keepdims=True), jnp.max(sn, axis=0, keepdims=True))
        pw = jnp.exp(sw - mw)
        pn = jnp.exp(sn - mw)
        lw = jnp.sum(pw, axis=0, keepdims=True) + jnp.sum(pn, axis=0, keepdims=True)
        ow = _dot_tn(pw.astype(BF16), vw) + _dot_tn(pn.astype(BF16), wn[:, 128:256].astype(BF16))
        ow_ref[...] = ow / to_rows(lw)

        m_ref[...] = jnp.full(m_ref.shape, NEG_INF, F32)
        l_ref[...] = jnp.zeros(l_ref.shape, F32)
        acc_ref[...] = jnp.zeros(acc_ref.shape, F32)

    qb = qb_ref[...]
    kt = jnp.concatenate([pg[0, 0].T.astype(BF16) for pg in pages], axis=0)
    vt = jnp.concatenate([pg[0, 1].T.astype(BF16) for pg in pages], axis=0)
    blk_per_step = PG * PAGE_SIZE // SEL_BLOCK
    sb = selb_ref[pl.ds(pl.multiple_of(j * blk_per_step, blk_per_step), blk_per_step), :]
    sb = jnp.broadcast_to(sb[:, None, :], (blk_per_step, SEL_BLOCK, NQ)).reshape(PG * PAGE_SIZE, NQ)
    sc = _dot_nt(kt, qb) + sb
    m_new, l_new, a, pv = _col_softmax_update(sc, m_ref[0:1, :], l_ref[0:1, :], None, vt)
    m_ref[0:1, :] = m_new
    l_ref[0:1, :] = l_new
    acc_ref[...] = to_rows(a) * acc_ref[...] + pv

    @pl.when(j == pl.num_programs(1) - 1)
    def _():
        zero = jnp.zeros((T, 256), F32)
        sn_rows = jnp.concatenate([snew_ref[...], zero], axis=0)
        kposn = past_len + lax.broadcasted_iota(jnp.int32, (2 * T, 1), 0)
        n_past_blk = past_len // SEL_BLOCK
        sbn = selb_ref[n_past_blk:n_past_blk + 1, :]
        sc = _dot_nt(sn_rows[:, 0:128].astype(BF16), qb) + sbn
        sc = jnp.where(kposn <= tl, sc, NEG_INF)
        m_new, l_new, a, pv = _col_softmax_update(sc, m_ref[0:1, :], l_ref[0:1, :], None,
                                                  sn_rows[:, 128:256].astype(BF16))
        o_sel = (to_rows(a) * acc_ref[...] + pv) / to_rows(l_new)
        gates = gate_ref[...].reshape(NQ, LANES)
        o = gates[:, 0:1] * oc_ref[...] + gates[:, 1:2] * o_sel + gates[:, 2:3] * ow_ref[...]
        low = lax.broadcasted_iota(jnp.int32, (T, LANES), 1) < NSA_HD
        for hp in range(NSA_HEADS // 2):
            a0 = o[(2 * hp) * T:(2 * hp + 1) * T]
            a1 = o[(2 * hp + 1) * T:(2 * hp + 2) * T]
            if 2 * hp < NSA_HPG:
                piece = jnp.where(low, a0, pltpu.roll(a1, NSA_HD, 1))
            else:
                piece = jnp.where(low, pltpu.roll(a0, NSA_HD, 1), a1)
            o_ref[:, hp * LANES:(hp + 1) * LANES] = piece


def _nsa_attn_sample(q, gates, ckv, ovt, cache_sel_t, page_table, state_win_t, win_new, sel_new, T, PG):
    B, n_pages = page_table.shape
    past_len = n_pages * PAGE_SIZE
    n_rows = ckv.shape[1]
    wb = state_win_t.shape[3]
    n_cmp = (past_len + T) // CMP_STRIDE - 1
    n_sel = past_len // SEL_BLOCK + -(-T // SEL_BLOCK)
    nb = ovt.shape[0]
    NQ = NSA_HEADS * T
    body = functools.partial(_nsa_attn_sample_body, T=T, PG=PG, past_len=past_len, n_cmp=n_cmp,
                             n_sel=n_sel, n_rows=n_rows, wb=wb)
    page_specs = [pl.BlockSpec((1, 2, LANES, PAGE_SIZE), functools.partial(
        lambda b, j, pt, k: (pt[b, j * PG + k], 0, 0, 0), k=k)) for k in range(PG)]
    grid_spec = pltpu.PrefetchScalarGridSpec(
        num_scalar_prefetch=1, grid=(B, n_pages // PG),
        in_specs=page_specs + [
            pl.BlockSpec((NSA_HEADS, T, LANES), lambda b, j, pt: (0, b, 0)),
            pl.BlockSpec((NSA_HEADS, T, LANES), lambda b, j, pt: (0, b, 0)),
            pl.BlockSpec((1, n_rows, 256), lambda b, j, pt: (b, 0, 0)),
            pl.BlockSpec((nb, n_rows), lambda b, j, pt: (0, 0)),
            pl.BlockSpec((1, 2, LANES, wb), lambda b, j, pt: (b, 0, 0, 0)),
            pl.BlockSpec((T, 256), lambda b, j, pt: (b, 0)),
            pl.BlockSpec((T, 256), lambda b, j, pt: (b, 0))],
        out_specs=pl.BlockSpec((T, NSA_HEADS * NSA_HD), lambda b, j, pt: (b, 0)),
        scratch_shapes=[pltpu.VMEM((NQ, LANES), BF16), pltpu.VMEM((nb, NQ), F32),
                        pltpu.VMEM((8, NQ), F32), pltpu.VMEM((8, NQ), F32),
                        pltpu.VMEM((NQ, LANES), F32), pltpu.VMEM((NQ, LANES), F32),
                        pltpu.VMEM((NQ, LANES), F32)])
    return pl.pallas_call(
        body, grid_spec=grid_spec,
        out_shape=jax.ShapeDtypeStruct((B * T, NSA_HEADS * NSA_HD), F32),
        compiler_params=_cp("parallel", "arbitrary"), name="nsa_attn_sample")(
            page_table, *([cache_sel_t] * PG), q, gates, ckv, ovt, state_win_t, win_new, sel_new)


def _rope_tables(pos, pattern):
    half = NSA_HD // 2
    inv = ROPE_THETA ** (-jnp.arange(half, dtype=F32) / half)
    ang = pos.astype(F32)[:, None] * inv[None, :]
    cos, sin = jnp.cos(ang), jnp.sin(ang)
    zero = jnp.zeros_like(cos)
    cs = [jnp.concatenate([cos, cos], 1) if p else jnp.concatenate([zero, zero], 1) for p in pattern]
    sn = [jnp.concatenate([-sin, sin], 1) if p else jnp.concatenate([zero, zero], 1) for p in pattern]
    return jnp.concatenate(cs, 1), jnp.concatenate(sn, 1)


def _overlap(n_rows, n_sel):
    ci = jnp.arange(n_rows, dtype=jnp.int32)[:, None] * CMP_STRIDE
    sj = jnp.arange(n_sel, dtype=jnp.int32)[None, :] * SEL_BLOCK
    return ((ci < sj + SEL_BLOCK) & (ci + CMP_LEN > sj)).astype(F32)


def _prep_weights(w):
    d = D_MODEL
    hk, hv = GLA_HEADS * GLA_DK, GLA_HEADS * GLA_DV
    p = {}
    p["ln_g"] = w["ln_g"].reshape(DEPTH, 2, 1, d)
    p["ln_b"] = w["ln_b"].reshape(DEPTH, 2, 1, d)
    w_in = w["gla_w_in"][0]
    p["gla_w_qkvr"] = w_in[:, :2 * hk + 2 * hv].astype(BF16)
    p["gla_w_a"] = jnp.pad(w_in[:, 2 * hk + 2 * hv:], ((0, 0), (0, LANES - GLA_GATE_RANK))).astype(BF16)
    p["gla_w_a2"] = jnp.pad(w["gla_w_a2"][0], ((0, LANES - GLA_GATE_RANK), (0, 0))).astype(BF16)
    p["gla_b_a"] = w["gla_b_a"][0].reshape(1, hk)
    p["gla_norm_g"] = w["gla_norm_g"][0].reshape(1, GLA_DV)
    p["gla_w_o"] = w["gla_w_o"][0].astype(BF16)
    p["nsa_w_kv"] = w["nsa_w_kv"].astype(BF16)
    nw = w["nsa_w_in"][0]
    wq = nw[:, :NSA_HEADS * NSA_HD].reshape(d, NSA_HEADS, NSA_HD)
    p["nsa_w_q"] = jnp.pad(wq, ((0, 0), (0, 0), (0, LANES - NSA_HD))).reshape(d, NSA_HEADS * LANES).astype(BF16)
    wg = nw[:, NSA_HEADS * NSA_HD:].reshape(d, 3, NSA_GROUPS, NSA_HPG)
    wg_grp = jnp.transpose(wg, (2, 0, 1, 3)).reshape(NSA_GROUPS, d, 3 * NSA_HPG)
    p["nsa_w_gate_grp"] = jnp.pad(wg_grp, ((0, 0), (0, 0), (0, LANES - 3 * NSA_HPG))).astype(BF16)
    wg_head = jnp.transpose(wg, (2, 3, 0, 1)).reshape(NSA_HEADS, d, 3)
    p["nsa_w_gate_head"] = jnp.pad(wg_head, ((0, 0), (0, 0), (0, LANES - 3))).astype(BF16)
    p["nsa_w_o"] = w["nsa_w_o"][0].astype(BF16)
    cw = {}
    for name, pe, w1, w2 in (("k", w["cmp_pe_k"], w["cmp_w1_k"], w["cmp_w2_k"]),
                             ("v", w["cmp_pe_v"], w["cmp_w1_v"], w["cmp_w2_v"])):
        half = CMP_STRIDE * NSA_HD
        cw["w1" + name] = jnp.concatenate([w1[:half], w1[half:]], axis=1).astype(BF16)
        cw["w2" + name] = jnp.pad(w2, ((0, 0), (0, LANES - NSA_HD))).astype(BF16)
        cw["w2" + name + "_lo"] = cw["w2" + name]
        pe2 = pe.reshape(2, 1, half)
        cw["pe" + name] = jnp.broadcast_to(pe2, (2, 16, half)).astype(BF16)
    p["cmp"] = cw
    p["ffn_w_gate"] = w["ffn_w_gate"][0].astype(BF16)
    p["ffn_w_up"] = w["ffn_w_up"][0].astype(BF16)
    p["ffn_w_down"] = w["ffn_w_down"][0].astype(BF16)
    p["moe_w_router"] = jnp.pad(w["moe_w_router"][0], ((0, 0), (0, LANES - N_EXPERTS)))
    p["moe_w_gate"] = w["moe_w_gate"][0].astype(BF16)
    p["moe_w_up"] = w["moe_w_up"][0].astype(BF16)
    p["moe_w_down"] = w["moe_w_down"][0].astype(BF16)
    p["ple_w_proj"] = w["ple_w_proj"].astype(BF16)
    p["ple_w_gate"] = w["ple_w_gate"].astype(BF16)
    return p


def _tile(M, cap):
    t = cap
    while M % t:
        t //= 2
    return t


def _trunk(x, ple, pos0, gla_s0, past, p):
    B, T, d = x.shape
    M = B * T
    tm = _tile(M, 1024)
    tm_s = _tile(M, 512)
    xf = x.reshape(M, d)
    xb = xf.astype(BF16)
    pleb = ple.reshape(DEPTH, M, PLE_DIM).astype(BF16)

    proj = _proj(xb, p["gla_w_qkvr"], tm, 512, F32)
    la = _gla_gate(xb, p["gla_w_a"], p["gla_w_a2"], p["gla_b_a"], tm)
    if T >= GLA_CHUNK:
        C, NC, Tp = GLA_CHUNK, _tile(T // GLA_CHUNK, 4), T
    else:
        C, NC, Tp = 16, 1, 16
        pad = lambda a: jnp.pad(a.reshape(B, T, -1), ((0, 0), (0, Tp - T), (0, 0))).reshape(B * Tp, -1)
        proj, la = pad(proj), pad(la)
    o, s_new = _gla(proj, la, p["gla_norm_g"], gla_s0, B, Tp, C, NC)
    if Tp != T:
        o = o.reshape(B, Tp, -1)[:, :T].reshape(M, -1)
    x1f, x1b = _mm_ln(o, p["gla_w_o"], xf, p["ln_g"][0, 0], p["ln_b"][0, 0], tm_s)
    x2f, x2b = _ffn(x1b, x1f, p["ffn_w_gate"], p["ffn_w_up"], p["ffn_w_down"],
                    p["ln_g"][0, 1], p["ln_b"][0, 1], tm, 256)
    x3f, x3b = _ple(x2b, x2f, pleb[0], p["ple_w_gate"][0], p["ple_w_proj"][0], tm_s)

    qpos = pos0 + jnp.arange(T, dtype=jnp.int32)
    cos_kv, sin_kv = _rope_tables(qpos, (1, 1))
    cos_q, sin_q = _rope_tables(qpos, (1, 0))
    if past is None:
        tq = _tile(T, 512)
    else:
        tq = _tile(M, 512)
        cos_kv, sin_kv, cos_q, sin_q = (jnp.tile(a, (B, 1)) for a in (cos_kv, sin_kv, cos_q, sin_q))
    kv = _nsa_kv(x3b, p["nsa_w_kv"], cos_kv, sin_kv, tq, past is None)
    cmp_rows = kv[0]
    from_t = lambda a: jnp.transpose(a.reshape(B, 2, NSA_GROUPS, NSA_HD, a.shape[-1]), (0, 4, 1, 2, 3))
    to_t = lambda a: jnp.transpose(a, (0, 2, 3, 4, 1)).reshape(a.shape[0], 2, NSA_GROUPS * NSA_HD, a.shape[1])
    shp = (B, T, 2, NSA_GROUPS, NSA_HD)
    if past is None:
        n_ch = T // CMP_STRIDE
        n_cmp = n_ch - 1
        n_sel = T // SEL_BLOCK
        cpos = jnp.arange(n_ch, dtype=jnp.int32) * CMP_STRIDE
        cos_c, sin_c = _rope_tables(cpos, (1, 0))
        ckv = _compress_prompt(cmp_rows.reshape(B * n_ch, CMP_STRIDE * 256), p["cmp"], cos_c, sin_c, B, n_ch)
        ov = jnp.pad(_overlap(n_ch, n_sel).T, ((NSA_HD, LANES - NSA_HD - n_sel), (0, 0))).astype(BF16)
        q = _nsa_q(x3b, p["nsa_w_q"], cos_q, sin_q, tq, BF16)
        gates = _nsa_gate(x3b, p["nsa_w_gate_grp"], tm_s)
        o = _nsa_attn_prompt(q, gates, ckv, ov, kv[1], B, T, _tile(T, 128), _tile(T, 512), n_cmp, n_sel)
        cmp_out, sel_out = from_t(kv[2]), from_t(kv[3])
        win_out = from_t(kv[4][..., T - min(WINDOW, T):])
    else:
        cache_cmp, cache_sel, page_table, state_win = past
        n_pool = cache_cmp.shape[0]
        n_pages = page_table.shape[1]
        past_len = n_pages * PAGE_SIZE
        PG = _tile(n_pages, 16)
        n_rows = past_len // CMP_STRIDE
        n_sel = past_len // SEL_BLOCK + -(-T // SEL_BLOCK)
        cpos = jnp.arange(n_rows, dtype=jnp.int32) * CMP_STRIDE
        cos_c, sin_c = _rope_tables(cpos, (1, 1))
        sel_rows, win_rows = kv[1], kv[2]
        ckv = _compress_sample(to_t(cache_cmp), page_table, p["cmp"], cos_c, sin_c, PG)
        nb = -(-n_sel // LANES) * LANES
        ovt = jnp.pad(_overlap(n_rows, n_sel).T, ((0, nb - n_sel), (0, 0))).astype(BF16)
        assert T < CMP_STRIDE and T <= SEL_BLOCK
        q = _nsa_q(x3b, p["nsa_w_q"], cos_q, sin_q, tq, F32)
        gates = _nsa_gate(x3b, p["nsa_w_gate_head"], tm_s)
        state_win_t = to_t(state_win)
        o = _nsa_attn_sample(q, gates, ckv, ovt, to_t(cache_sel), page_table,
                             state_win_t, win_rows, sel_rows, T, PG).astype(BF16)
        cmp_out, sel_out = cmp_rows.reshape(shp), sel_rows.reshape(shp)
        win_out = from_t(jnp.concatenate([state_win_t, to_t(win_rows.reshape(shp))], axis=-1)[..., T:])
    x4f, x4b = _mm_ln(o, p["nsa_w_o"], x3f, p["ln_g"][1, 0], p["ln_b"][1, 0], tm_s)
    tb = _tile(M, 1024)
    nb, LT, n_tiles, n_groups = _moe_sizes(M, tb)
    info, cnt = _router(x4f, p["moe_w_router"], tb)
    plan = _moe_plan(cnt[:, 0, :N_EXPERTS].astype(jnp.int32), LT, n_tiles, n_groups)
    xs = _moe_dispatch(x4b, info, plan, tb, n_tiles)
    ys = _moe_ffn(xs, plan, p["moe_w_gate"], p["moe_w_up"], p["moe_w_down"], n_groups, 512)
    x5f, x5b = _moe_combine(ys, info, x4f, plan, p["ln_g"][1, 1], p["ln_b"][1, 1], tb)
    x6f, _ = _ple(x5b, x5f, pleb[1], p["ple_w_gate"][1], p["ple_w_proj"][1], tm_s)

    return (x6f.reshape(B, T, d), s_new[None], cmp_out, sel_out, win_out)


def kernel(x_prompt, x_sample, state_gla, cache_cmp, cache_sel, state_win, page_table, p_prompt, p_sample,
           ln_g, ln_b, gla_w_in, gla_w_a2, gla_b_a, gla_norm_g, gla_w_o,
           nsa_w_kv, cmp_pe_k, cmp_w1_k, cmp_w2_k, cmp_pe_v, cmp_w1_v, cmp_w2_v,
           nsa_w_in, nsa_w_o, ffn_w_gate, ffn_w_up, ffn_w_down,
           moe_w_router, moe_w_gate, moe_w_up, moe_w_down, ple_w_proj, ple_w_gate):
    w = dict(ln_g=ln_g, ln_b=ln_b, gla_w_in=gla_w_in, gla_w_a2=gla_w_a2, gla_b_a=gla_b_a,
             gla_norm_g=gla_norm_g, gla_w_o=gla_w_o, nsa_w_kv=nsa_w_kv,
             cmp_pe_k=cmp_pe_k, cmp_w1_k=cmp_w1_k, cmp_w2_k=cmp_w2_k,
             cmp_pe_v=cmp_pe_v, cmp_w1_v=cmp_w1_v, cmp_w2_v=cmp_w2_v,
             nsa_w_in=nsa_w_in, nsa_w_o=nsa_w_o, ffn_w_gate=ffn_w_gate, ffn_w_up=ffn_w_up,
             ffn_w_down=ffn_w_down, moe_w_router=moe_w_router, moe_w_gate=moe_w_gate,
             moe_w_up=moe_w_up, moe_w_down=moe_w_down, ple_w_proj=ple_w_proj, ple_w_gate=ple_w_gate)
    p = _prep_weights(w)
    Bp = x_prompt.shape[0]
    gla_zero = jnp.zeros((Bp, GLA_HEADS, GLA_DK, GLA_DV), F32)
    yp, gp, cp, sp, wp = _trunk(x_prompt, p_prompt, 0, gla_zero, None, p)
    past_len = page_table.shape[1] * PAGE_SIZE
    ys, gs, cs, ss, ws = _trunk(x_sample, p_sample, past_len, state_gla[0],
                                (cache_cmp, cache_sel, page_table, state_win), p)
    return (yp, ys, gp, gs, cp, cs, sp, ss, wp, ws)
```

```python
import functools
import math

import jax
import jax.numpy as jnp
from jax import lax
from jax.experimental import pallas as pl
from jax.experimental.pallas import tpu as pltpu

F32 = jnp.float32
BF16 = jnp.bfloat16

D_MODEL = 1024
DEPTH = 2
PAGE_SIZE = 128
PLE_DIM = 256
DEEPNORM_ALPHA = (2.0 * DEPTH) ** 0.25
LN_EPS = 1e-5
ROPE_THETA = 10000.0

GLA_HEADS = 4
GLA_DK = 128
GLA_DV = 256
GLA_GATE_RANK = 16
GLA_TAU = 16.0
GLA_CHUNK = 64

NSA_HEADS = 16
NSA_HD = 64
NSA_GROUPS = 2
NSA_HPG = 8
CMP_STRIDE = 16
CMP_LEN = 32
CMP_HIDDEN = 128
SEL_BLOCK = 64
SEL_TOP = 16
WINDOW = 512
NEG_INF = -1e30
FORCE_SCORE = 1e9

N_EXPERTS = 8
LANES = 128
VMEM_LIMIT = 56 * 1024 * 1024

_NT = (((1,), (1,)), ((), ()))
_TN = (((0,), (0,)), ((), ()))


def _cp(*sem):
    return pltpu.CompilerParams(dimension_semantics=sem, vmem_limit_bytes=VMEM_LIMIT)


def _dot(a, b):
    return jnp.dot(a, b, preferred_element_type=F32)


def _dot_nt(a, b):
    return lax.dot_general(a, b, _NT, preferred_element_type=F32)


def _dot_tn(a, b):
    return lax.dot_general(a, b, _TN, preferred_element_type=F32)


def _split2(x):
    hi = x.astype(BF16)
    lo = (x - hi.astype(F32)).astype(BF16)
    return hi, lo


def _split3(x):
    hi = x.astype(BF16)
    r = x - hi.astype(F32)
    mid = r.astype(BF16)
    lo = (r - mid.astype(F32)).astype(BF16)
    return hi, mid, lo


def _vmod(x, n):
    assert n & (n - 1) == 0
    return x & (n - 1)


def _vdiv(x, n):
    assert n & (n - 1) == 0
    return x >> (n.bit_length() - 1)


def _layer_norm(y, g, b):
    mu = jnp.mean(y, axis=-1, keepdims=True)
    yc = y - mu
    var = jnp.mean(yc * yc, axis=-1, keepdims=True)
    return yc * lax.rsqrt(var + LN_EPS) * g + b


def _rope_lanes(x, cos, sin):
    lane = lax.broadcasted_iota(jnp.int32, x.shape, x.ndim - 1)
    first = _vmod(lane, NSA_HD) < (NSA_HD // 2)
    partner = jnp.where(first, pltpu.roll(x, LANES - NSA_HD // 2, x.ndim - 1),
                        pltpu.roll(x, NSA_HD // 2, x.ndim - 1))
    return x * cos + partner * sin


def _proj_body(x_ref, w_ref, o_ref):
    o_ref[...] = _dot(x_ref[...], w_ref[...]).astype(o_ref.dtype)


def _proj(x, w, tm, tn, out_dtype):
    M, K = x.shape
    N = w.shape[1]
    return pl.pallas_call(
        _proj_body, grid=(M // tm, N // tn),
        in_specs=[pl.BlockSpec((tm, K), lambda i, j: (i, 0)),
                  pl.BlockSpec((K, tn), lambda i, j: (0, j))],
        out_specs=pl.BlockSpec((tm, tn), lambda i, j: (i, j)),
        out_shape=jax.ShapeDtypeStruct((M, N), out_dtype),
        compiler_params=_cp("parallel", "arbitrary"), name="proj")(x, w)


def _gla_gate_body(x_ref, wa_ref, wa2_ref, ba_ref, o_ref):
    a = _dot(x_ref[...], wa_ref[...])
    z = _dot(a.astype(BF16), wa2_ref[...]) + ba_ref[...]
    o_ref[...] = (jnp.minimum(z, 0.0) - jnp.log1p(jnp.exp(-jnp.abs(z)))) * (1.0 / GLA_TAU)


def _gla_gate(xb, wa, wa2, ba, tm):
    M = xb.shape[0]
    N = wa2.shape[1]
    return pl.pallas_call(
        _gla_gate_body, grid=(M // tm,),
        in_specs=[pl.BlockSpec((tm, D_MODEL), lambda i: (i, 0)),
                  pl.BlockSpec((D_MODEL, LANES), lambda i: (0, 0)),
                  pl.BlockSpec((LANES, N), lambda i: (0, 0)),
                  pl.BlockSpec((1, N), lambda i: (0, 0))],
        out_specs=pl.BlockSpec((tm, N), lambda i: (i, 0)),
        out_shape=jax.ShapeDtypeStruct((M, N), F32),
        compiler_params=_cp("parallel"), name="gla_gate")(xb, wa, wa2, ba)


def _gla_body(q_ref, k_ref, v_ref, r_ref, la_ref, g_ref, s0_ref, o_ref, sout_ref, s_ref, *, C, NC):
    t = pl.program_id(1)

    @pl.when(t == 0)
    def _():
        s_ref[...] = s0_ref[0]

    row = lax.broadcasted_iota(jnp.int32, (C, C), 0)
    col = lax.broadcasted_iota(jnp.int32, (C, C), 1)
    tri = col <= row
    ones = jnp.ones((C, LANES), BF16)
    rowk = lax.broadcasted_iota(jnp.int32, (C, GLA_DK), 0)
    rt = lax.broadcasted_iota(jnp.int32, (C * NC, C * NC), 0)
    ct = lax.broadcasted_iota(jnp.int32, (C * NC, C * NC), 1)
    tri_bd = ((ct <= rt) & (_vdiv(rt, C) == _vdiv(ct, C))).astype(BF16)
    la_hi, la_mid, la_lo = _split3(la_ref[...])
    b_all = _dot(tri_bd, la_hi) + _dot(tri_bd, la_mid) + _dot(tri_bd, la_lo)
    for h in range(GLA_HEADS):
        kc = slice(h * GLA_DK, (h + 1) * GLA_DK)
        vc = slice(h * GLA_DV, (h + 1) * GLA_DV)
        state = s_ref[h]
        for c in range(NC):
            sl = pl.ds(c * C, C)
            rs = slice(c * C, (c + 1) * C)
            b = b_all[rs, kc]
            tot = (_dot_tn(la_hi[rs, kc], ones) + _dot_tn(la_mid[rs, kc], ones)
                   + _dot_tn(la_lo[rs, kc], ones))
            b_mid = jnp.sum(jnp.where(rowk == (C - 1) // 2, b, 0.0), axis=0, keepdims=True)
            b_last = jnp.sum(jnp.where(rowk == C - 1, b, 0.0), axis=0, keepdims=True)
            q = q_ref[sl, kc] * (GLA_DK ** -0.5)
            k = k_ref[sl, kc]
            v = v_ref[sl, vc].astype(BF16)
            qe = (q * jnp.exp(b - b_mid)).astype(BF16)
            ke = (k * jnp.exp(b_mid - b)).astype(BF16)
            a = jnp.where(tri, _dot_nt(qe, ke), 0.0)
            o = _dot(a.astype(BF16), v) + _dot((q * jnp.exp(b)).astype(BF16), state.astype(BF16))
            kd = (k * jnp.exp(b_last - b)).astype(BF16)
            state = jnp.exp(tot[:, 0:1]) * state + _dot_tn(kd, v)
            ms = jnp.mean(o * o, axis=-1, keepdims=True)
            rr = r_ref[sl, vc]
            o = o * lax.rsqrt(ms + LN_EPS) * g_ref[...] * (rr * jax.nn.sigmoid(rr))
            o_ref[sl, vc] = o.astype(o_ref.dtype)
        s_ref[h] = state

    @pl.when(t == pl.num_programs(1) - 1)
    def _():
        sout_ref[0] = s_ref[...]


def _gla(proj, la, norm_g, s0, B, T, C, NC):
    Tt = C * NC
    nT = T // Tt
    M = B * T
    hk, hv = GLA_HEADS * GLA_DK, GLA_HEADS * GLA_DV
    body = functools.partial(_gla_body, C=C, NC=NC)
    state_spec = pl.BlockSpec((1, GLA_HEADS, GLA_DK, GLA_DV), lambda b, t: (b, 0, 0, 0))
    return pl.pallas_call(
        body, grid=(B, nT),
        in_specs=[pl.BlockSpec((Tt, hk), lambda b, t: (b * nT + t, 0)),
                  pl.BlockSpec((Tt, hk), lambda b, t: (b * nT + t, 1)),
                  pl.BlockSpec((Tt, hv), lambda b, t: (b * nT + t, 2 * hk // hv)),
                  pl.BlockSpec((Tt, hv), lambda b, t: (b * nT + t, 2 * hk // hv + 1)),
                  pl.BlockSpec((Tt, hk), lambda b, t: (b * nT + t, 0)),
                  pl.BlockSpec((1, GLA_DV), lambda b, t: (0, 0)),
                  state_spec],
        out_specs=[pl.BlockSpec((Tt, hv), lambda b, t: (b * nT + t, 0)), state_spec],
        out_shape=[jax.ShapeDtypeStruct((M, hv), BF16),
                   jax.ShapeDtypeStruct((B, GLA_HEADS, GLA_DK, GLA_DV), F32)],
        scratch_shapes=[pltpu.VMEM((GLA_HEADS, GLA_DK, GLA_DV), F32)],
        compiler_params=_cp("parallel", "arbitrary"), name="gla")(
            proj, proj, proj, proj, la, norm_g, s0)


def _mm_ln_body(a_ref, w_ref, r_ref, g_ref, b_ref, of_ref, ob_ref):
    y = DEEPNORM_ALPHA * r_ref[...] + _dot(a_ref[...], w_ref[...])
    out = _layer_norm(y, g_ref[...], b_ref[...])
    of_ref[...] = out
    ob_ref[...] = out.astype(BF16)


def _mm_ln(a, w, resid, g, b, tm):
    M, K = a.shape
    N = w.shape[1]
    return pl.pallas_call(
        _mm_ln_body, grid=(M // tm,),
        in_specs=[pl.BlockSpec((tm, K), lambda i: (i, 0)),
                  pl.BlockSpec((K, N), lambda i: (0, 0)),
                  pl.BlockSpec((tm, N), lambda i: (i, 0)),
                  pl.BlockSpec((1, N), lambda i: (0, 0)),
                  pl.BlockSpec((1, N), lambda i: (0, 0))],
        out_specs=[pl.BlockSpec((tm, N), lambda i: (i, 0)),
                   pl.BlockSpec((tm, N), lambda i: (i, 0))],
        out_shape=[jax.ShapeDtypeStruct((M, N), F32), jax.ShapeDtypeStruct((M, N), BF16)],
        compiler_params=_cp("parallel"), name="mm_ln")(a, w, resid, g, b)


def _ffn_body(xb_ref, xf_ref, wg_ref, wu_ref, wd_ref, g_ref, b_ref, of_ref, ob_ref, acc_ref):
    j = pl.program_id(1)

    @pl.when(j == 0)
    def _():
        acc_ref[...] = jnp.zeros_like(acc_ref)

    x = xb_ref[...]
    hg = _dot(x, wg_ref[...])
    hu = _dot(x, wu_ref[...])
    h = (hg * jax.nn.sigmoid(hg)) * hu
    acc_ref[...] += _dot(h.astype(BF16), wd_ref[...])

    @pl.when(j == pl.num_programs(1) - 1)
    def _():
        out = _layer_norm(DEEPNORM_ALPHA * xf_ref[...] + acc_ref[...], g_ref[...], b_ref[...])
        of_ref[...] = out
        ob_ref[...] = out.astype(BF16)


def _ffn(xb, xf, wg, wu, wd, g, b, tm, tf):
    M = xb.shape[0]
    F = wg.shape[1]
    return pl.pallas_call(
        _ffn_body, grid=(M // tm, F // tf),
        in_specs=[pl.BlockSpec((tm, D_MODEL), lambda i, j: (i, 0)),
                  pl.BlockSpec((tm, D_MODEL), lambda i, j: (i, 0)),
                  pl.BlockSpec((D_MODEL, tf), lambda i, j: (0, j)),
                  pl.BlockSpec((D_MODEL, tf), lambda i, j: (0, j)),
                  pl.BlockSpec((tf, D_MODEL), lambda i, j: (j, 0)),
                  pl.BlockSpec((1, D_MODEL), lambda i, j: (0, 0)),
                  pl.BlockSpec((1, D_MODEL), lambda i, j: (0, 0))],
        out_specs=[pl.BlockSpec((tm, D_MODEL), lambda i, j: (i, 0)),
                   pl.BlockSpec((tm, D_MODEL), lambda i, j: (i, 0))],
        out_shape=[jax.ShapeDtypeStruct((M, D_MODEL), F32), jax.ShapeDtypeStruct((M, D_MODEL), BF16)],
        scratch_shapes=[pltpu.VMEM((tm, D_MODEL), F32)],
        compiler_params=_cp("parallel", "arbitrary"), name="ffn")(xb, xf, wg, wu, wd, g, b)


def _ple_body(xb_ref, xf_ref, pb_ref, wg_ref, wp_ref, of_ref, ob_ref):
    gate = jax.nn.sigmoid(_dot(xb_ref[...], wg_ref[...]))
    out = xf_ref[...] + gate * _dot(pb_ref[...], wp_ref[...])
    of_ref[...] = out
    ob_ref[...] = out.astype(BF16)


def _ple(xb, xf, pb, wg, wp, tm):
    M = xb.shape[0]
    return pl.pallas_call(
        _ple_body, grid=(M // tm,),
        in_specs=[pl.BlockSpec((tm, D_MODEL), lambda i: (i, 0)),
                  pl.BlockSpec((tm, D_MODEL), lambda i: (i, 0)),
                  pl.BlockSpec((tm, PLE_DIM), lambda i: (i, 0)),
                  pl.BlockSpec((D_MODEL, D_MODEL), lambda i: (0, 0)),
                  pl.BlockSpec((PLE_DIM, D_MODEL), lambda i: (0, 0))],
        out_specs=[pl.BlockSpec((tm, D_MODEL), lambda i: (i, 0)),
                   pl.BlockSpec((tm, D_MODEL), lambda i: (i, 0))],
        out_shape=[jax.ShapeDtypeStruct((M, D_MODEL), F32), jax.ShapeDtypeStruct((M, D_MODEL), BF16)],
        compiler_params=_cp("parallel"), name="ple")(xb, xf, pb, wg, wp)


def _router_body(x_ref, w_ref, info_ref, info_t_ref, cnt_ref):
    x = x_ref[...]
    xh, xl = _split2(x)
    w = w_ref[...]
    wh, wl = _split2(w)
    logits = _dot(xh, wh) + _dot(xl, wh) + _dot(xh, wl)
    lane = lax.broadcasted_iota(jnp.int32, logits.shape, 1)
    live = lane < N_EXPERTS
    lg = jnp.where(live, logits, -jnp.inf)
    m1 = jnp.max(lg, axis=-1, keepdims=True)
    i1 = jnp.min(jnp.where(lg == m1, lane, LANES), axis=-1, keepdims=True)
    lg2 = jnp.where(lane == i1, -jnp.inf, lg)
    m2 = jnp.max(lg2, axis=-1, keepdims=True)
    i2 = jnp.min(jnp.where(lg2 == m2, lane, LANES), axis=-1, keepdims=True)
    e2 = jnp.exp(m2 - m1)
    den = 1.0 + e2
    onehot = jnp.where((lane == i1) | (lane == i2), 1.0, 0.0)
    tb = x.shape[0]
    before = (lax.broadcasted_iota(jnp.int32, (tb, tb), 1) < lax.broadcasted_iota(jnp.int32, (tb, tb), 0))
    cum = _dot(before.astype(BF16), onehot.astype(BF16))
    r1 = jnp.sum(jnp.where(lane == i1, cum, 0.0), axis=-1, keepdims=True)
    r2 = jnp.sum(jnp.where(lane == i2, cum, 0.0), axis=-1, keepdims=True)
    cols = (i1.astype(F32), i2.astype(F32), 1.0 / den, e2 / den, r1, r2)
    info = jnp.zeros(logits.shape, F32)
    for c, val in enumerate(cols):
        info = jnp.where(lane == c, val, info)
    info_ref[...] = info
    info_t_ref[0] = info.T[0:8]
    cnt_ref[0] = jnp.broadcast_to(jnp.sum(onehot, axis=0, keepdims=True), (8, LANES))


def _router(xf, w_pad, tb):
    M = xf.shape[0]
    return pl.pallas_call(
        _router_body, grid=(M // tb,),
        in_specs=[pl.BlockSpec((tb, D_MODEL), lambda i: (i, 0)),
                  pl.BlockSpec((D_MODEL, LANES), lambda i: (0, 0))],
        out_specs=[pl.BlockSpec((tb, LANES), lambda i: (i, 0)),
                   pl.BlockSpec((1, 8, tb), lambda i: (i, 0, 0)),
                   pl.BlockSpec((1, 8, LANES), lambda i: (i, 0, 0))],
        out_shape=[jax.ShapeDtypeStruct((M, LANES), F32),
                   jax.ShapeDtypeStruct((M // tb, 8, tb), F32),
                   jax.ShapeDtypeStruct((M // tb, 8, LANES), F32)],
        compiler_params=_cp("parallel"), name="router")(xf, w_pad)


MOE_TILE = 64
MOE_GROUP = 8


def _moe_sizes(M, tb):
    nb = M // tb
    LT = 2 * tb // MOE_TILE + N_EXPERTS
    n_tiles = -(-(nb * LT) // MOE_GROUP) * MOE_GROUP + N_EXPERTS * MOE_GROUP
    return nb, LT, n_tiles, n_tiles // MOE_GROUP


def _moe_plan(counts, LT, n_tiles, n_groups):
    nt = (counts + MOE_TILE - 1) // MOE_TILE
    lend = jnp.cumsum(nt, axis=1)
    lstart = lend - nt
    lt = jnp.arange(LT, dtype=jnp.int32)
    used = lt[None, :] < lend[:, -1:]
    e_of = jnp.minimum(jnp.sum(lt[None, :, None] >= lend[:, None, :], axis=-1), N_EXPERTS - 1).astype(jnp.int32)
    k_in = lt[None, :] - jnp.take_along_axis(lstart, e_of, axis=1)
    tot = jnp.sum(nt, axis=0)
    tot_pad = (tot + MOE_GROUP - 1) // MOE_GROUP * MOE_GROUP
    gend = jnp.cumsum(tot_pad)
    goff = gend - tot_pad
    boff = jnp.cumsum(nt, axis=0) - nt
    dt = goff[e_of] + jnp.take_along_axis(boff, e_of, axis=1) + k_in
    g4 = jnp.arange(n_groups, dtype=jnp.int32) * MOE_GROUP
    nu = (gend[-1] // MOE_GROUP).astype(jnp.int32)
    g4 = jnp.minimum(g4, gend[-1] - MOE_GROUP)
    ge = jnp.sum(g4[:, None] >= gend[None, :], axis=-1).astype(jnp.int32)
    nb = counts.shape[0]
    ids = (jnp.arange(nb, dtype=jnp.int32)[:, None] * LT + lt[None, :])
    src = jnp.full((n_tiles + 1,), -1, jnp.int32).at[jnp.where(used, dt, n_tiles).reshape(-1)].set(ids.reshape(-1))
    src = src[:n_tiles].reshape(n_groups, MOE_GROUP)
    gt = jnp.where(src < 0, src[:, 0:1], src)
    return dict(le=jnp.where(used, e_of, -1).astype(jnp.int32),
                lr=(k_in * MOE_TILE).astype(jnp.int32),
                dt=jnp.where(used, dt, 0).astype(jnp.int32),
                gt=jnp.maximum(gt, 0).astype(jnp.int32), ge=ge, nu=nu.reshape(1))


def _moe_dispatch_body(le_ref, lr_ref, x_ref, info_t_ref, o_ref, *, LT):
    b = pl.program_id(0)
    it = info_t_ref[0]
    i1, i2, r1, r2 = it[0:1], it[1:2], it[4:5], it[5:6]
    sub = lax.broadcasted_iota(jnp.int32, (MOE_TILE, 1), 0)
    tiles = []
    for lt in range(LT):
        ef = le_ref[b, lt].astype(F32)
        slot = (lr_ref[b, lt] + sub).astype(F32)
        hit = ((i1 == ef) & (r1 == slot)) | ((i2 == ef) & (r2 == slot))
        tiles.append(jnp.where(hit, 1.0, 0.0).astype(BF16))
    o_ref[...] = _dot(jnp.concatenate(tiles, axis=0), x_ref[...]).astype(BF16)


def _moe_dispatch(xb, info_t, plan, tb):
    nb, LT = plan["le"].shape
    grid_spec = pltpu.PrefetchScalarGridSpec(
        num_scalar_prefetch=2, grid=(nb,),
        in_specs=[pl.BlockSpec((tb, D_MODEL), lambda b, le, lr: (b, 0)),
                  pl.BlockSpec((1, 8, tb), lambda b, le, lr: (b, 0, 0))],
        out_specs=pl.BlockSpec((LT * MOE_TILE, D_MODEL), lambda b, le, lr: (b, 0)))
    return pl.pallas_call(
        functools.partial(_moe_dispatch_body, LT=LT), grid_spec=grid_spec,
        out_shape=jax.ShapeDtypeStruct((nb * LT * MOE_TILE, D_MODEL), BF16),
        compiler_params=_cp("arbitrary"), name="moe_dispatch")(plan["le"], plan["lr"], xb, info_t)


def _moe_ffn_body(gt_ref, ge_ref, nu_ref, *refs):
    x_refs = refs[:MOE_GROUP]
    wg_ref, wu_ref, wd_ref, o_ref, x_ref, acc_ref = refs[MOE_GROUP:]
    j = pl.program_id(1)

    @pl.when(pl.program_id(0) < nu_ref[0])
    def _():
        @pl.when(j == 0)
        def _():
            acc_ref[...] = jnp.zeros_like(acc_ref)
            for k in range(MOE_GROUP):
                x_ref[k * MOE_TILE:(k + 1) * MOE_TILE, :] = x_refs[k][...]

        x = x_ref[...]
        hg = _dot(x, wg_ref[...])
        hu = _dot(x, wu_ref[...])
        h = (hg * jax.nn.sigmoid(hg)) * hu
        acc_ref[...] += _dot(h.astype(BF16), wd_ref[...])

        @pl.when(j == pl.num_programs(1) - 1)
        def _():
            o_ref[...] = acc_ref[...].astype(BF16)

    @pl.when((pl.program_id(0) >= nu_ref[0]) & (j == 0))
    def _():
        o_ref[...] = jnp.zeros_like(o_ref)


def _moe_ffn(xs, plan, wg, wu, wd, n_groups, tf):
    tm = MOE_TILE * MOE_GROUP
    F = wg.shape[2]
    nf = F // tf
    live = lambda g, nu: jnp.minimum(g, nu[0] - 1)
    col = lambda g, j, nu: jnp.where(g < nu[0], j, nf - 1)
    x_specs = [pl.BlockSpec((MOE_TILE, D_MODEL), functools.partial(
        lambda g, j, gt, ge, nu, k: (gt[live(g, nu), k], 0), k=k)) for k in range(MOE_GROUP)]
    grid_spec = pltpu.PrefetchScalarGridSpec(
        num_scalar_prefetch=3, grid=(n_groups, nf),
        in_specs=x_specs + [
            pl.BlockSpec((None, D_MODEL, tf), lambda g, j, gt, ge, nu: (ge[g], 0, col(g, j, nu))),
            pl.BlockSpec((None, D_MODEL, tf), lambda g, j, gt, ge, nu: (ge[g], 0, col(g, j, nu))),
            pl.BlockSpec((None, tf, D_MODEL), lambda g, j, gt, ge, nu: (ge[g], col(g, j, nu), 0))],
        out_specs=pl.BlockSpec((tm, D_MODEL), lambda g, j, gt, ge, nu: (g, 0)),
        scratch_shapes=[pltpu.VMEM((tm, D_MODEL), BF16), pltpu.VMEM((tm, D_MODEL), F32)])
    return pl.pallas_call(
        _moe_ffn_body, grid_spec=grid_spec,
        out_shape=jax.ShapeDtypeStruct((n_groups * tm, D_MODEL), BF16),
        compiler_params=_cp("arbitrary", "arbitrary"), name="moe_ffn")(
            plan["gt"], plan["ge"], plan["nu"], *([xs] * MOE_GROUP), wg, wu, wd)


def _moe_combine_body(le_ref, lr_ref, dt_ref, *refs, LT, KC):
    y_refs = refs[:LT]
    info_ref, xf_ref, g_ref, b_ref, of_ref, ob_ref = refs[LT:]
    b = pl.program_id(0)
    info = info_ref[...]
    i1, i2, w1, w2, r1, r2 = (info[:, c:c + 1] for c in range(6))
    per = LANES // MOE_TILE
    lane = lax.broadcasted_iota(jnp.int32, (1, LANES), 1)
    moe = None
    for c0 in range(0, LT, KC):
        pw, ys = [], []
        for s0 in range(c0, c0 + KC, per):
            ef = jnp.zeros((1, LANES), F32)
            slot = jnp.zeros((1, LANES), F32)
            for k in range(per):
                here = (lane >= k * MOE_TILE) & (lane < (k + 1) * MOE_TILE)
                ef = jnp.where(here, le_ref[b, s0 + k].astype(F32), ef)
                slot = jnp.where(here, (lr_ref[b, s0 + k] - k * MOE_TILE + lane).astype(F32), slot)
            pw.append((jnp.where((i1 == ef) & (r1 == slot), w1, 0.0)
                       + jnp.where((i2 == ef) & (r2 == slot), w2, 0.0)).astype(BF16))
            ys.extend(y_refs[s0 + k][...] for k in range(per))
        part = _dot(jnp.concatenate(pw, axis=1), jnp.concatenate(ys, axis=0))
        moe = part if moe is None else moe + part
    out = _layer_norm(DEEPNORM_ALPHA * xf_ref[...] + moe, g_ref[...], b_ref[...])
    of_ref[...] = out
    ob_ref[...] = out.astype(BF16)


def _moe_combine(ys, info, xf, plan, g, b, tb):
    M = xf.shape[0]
    nb, LT = plan["le"].shape
    KC = 8 if LT % 8 == 0 else LT
    blk = lambda bb, le, lr, dt: (bb, 0)
    const = lambda bb, le, lr, dt: (0, 0)
    y_specs = [pl.BlockSpec((MOE_TILE, D_MODEL), functools.partial(
        lambda bb, le, lr, dt, k: (dt[bb, k], 0), k=k)) for k in range(LT)]
    grid_spec = pltpu.PrefetchScalarGridSpec(
        num_scalar_prefetch=3, grid=(nb,),
        in_specs=y_specs + [pl.BlockSpec((tb, LANES), blk),
                            pl.BlockSpec((tb, D_MODEL), blk),
                            pl.BlockSpec((1, D_MODEL), const),
                            pl.BlockSpec((1, D_MODEL), const)],
        out_specs=[pl.BlockSpec((tb, D_MODEL), blk), pl.BlockSpec((tb, D_MODEL), blk)])
    return pl.pallas_call(
        functools.partial(_moe_combine_body, LT=LT, KC=KC), grid_spec=grid_spec,
        out_shape=[jax.ShapeDtypeStruct((M, D_MODEL), F32), jax.ShapeDtypeStruct((M, D_MODEL), BF16)],
        compiler_params=_cp("arbitrary"), name="moe_combine")(
            plan["le"], plan["lr"], plan["dt"], *([ys] * LT), info, xf, g, b)


def _nsa_kv_body(s_ref, w_ref, cos_ref, sin_ref, cmp_ref, *rest, tm, nT, with_kvx):
    rows = _dot(s_ref[...], w_ref[...])
    cos = cos_ref[...]
    sin = sin_ref[...]
    cmp_ref[...] = rows[:, 0:256]
    ks = _rope_lanes(rows[:, 256:384], cos, sin)
    vs = rows[:, 384:512]
    kw = _rope_lanes(rows[:, 512:640], cos, sin)
    vw = rows[:, 640:768]
    if not with_kvx:
        sel_ref, win_ref = rest
        sel_ref[:, 0:128] = ks
        sel_ref[:, 128:256] = vs
        win_ref[:, 0:128] = kw
        win_ref[:, 128:256] = vw
    else:
        kvx_ref, cmp_t_ref, sel_t_ref, win_t_ref = rest
        cmp_t_ref[0, 0] = rows[:, 0:128].T
        cmp_t_ref[0, 1] = rows[:, 128:256].T
        sel_t_ref[0, 0] = ks.T
        sel_t_ref[0, 1] = vs.T
        win_t_ref[0, 0] = kw.T
        win_t_ref[0, 1] = vw.T
        lane = lax.broadcasted_iota(jnp.int32, (tm, LANES), 1)
        pos = (pl.program_id(0) % nT) * tm + lax.broadcasted_iota(jnp.int32, (tm, LANES), 0)
        low = lane < NSA_HD
        onehot = jnp.where(lane - NSA_HD == _vdiv(pos, SEL_BLOCK), 1.0, 0.0)
        for g in range(NSA_GROUPS):
            pick = (lambda x: x) if g == 0 else (lambda x: pltpu.roll(x, NSA_HD, 1))
            kvx_ref[g, :, 0:128] = jnp.where(low, pick(ks), onehot).astype(BF16)
            kvx_ref[g, :, 128:256] = jnp.where(low, pick(vs), 0.0).astype(BF16)
            kvx_ref[g, :, 256:384] = jnp.where(low, pick(kw), 0.0).astype(BF16)
            kvx_ref[g, :, 384:512] = jnp.where(low, pick(vw), 0.0).astype(BF16)


def _nsa_kv(sb, w, cos, sin, tm, with_kvx):
    M = sb.shape[0]
    nT = cos.shape[0] // tm
    body = functools.partial(_nsa_kv_body, tm=tm, nT=nT, with_kvx=with_kvx)
    if with_kvx:
        T = cos.shape[0]
        t_spec = pl.BlockSpec((1, 2, LANES, tm), lambda i: (i // nT, 0, 0, i % nT))
        t_shape = jax.ShapeDtypeStruct((M // T, 2, LANES, T), F32)
        out_specs = [pl.BlockSpec((tm, 256), lambda i: (i, 0)),
                     pl.BlockSpec((NSA_GROUPS, tm, 512), lambda i: (0, i, 0)), t_spec, t_spec, t_spec]
        out_shape = [jax.ShapeDtypeStruct((M, 256), F32),
                     jax.ShapeDtypeStruct((NSA_GROUPS, M, 512), BF16), t_shape, t_shape, t_shape]
    else:
        out_specs = [pl.BlockSpec((tm, 256), lambda i: (i, 0))] * 3
        out_shape = [jax.ShapeDtypeStruct((M, 256), F32)] * 3
    return pl.pallas_call(
        body, grid=(M // tm,),
        in_specs=[pl.BlockSpec((tm, D_MODEL), lambda i: (i, 0)),
                  pl.BlockSpec((D_MODEL, 768), lambda i: (0, 0)),
                  pl.BlockSpec((tm, LANES), lambda i: (i % nT, 0)),
                  pl.BlockSpec((tm, LANES), lambda i: (i % nT, 0))],
        out_specs=out_specs, out_shape=out_shape,
        compiler_params=_cp("parallel"), name="nsa_kv")(sb, w, cos, sin)


def _nsa_q_body(x_ref, w_ref, cos_ref, sin_ref, o_ref):
    res = _dot(x_ref[...], w_ref[...])
    cos = cos_ref[...]
    sin = sin_ref[...]
    for h in range(NSA_HEADS):
        xh = res[:, h * LANES:(h + 1) * LANES]
        o_ref[h] = (_rope_lanes(xh, cos, sin) * (NSA_HD ** -0.5)).astype(o_ref.dtype)


def _nsa_q(xb, w_pad, cos, sin, tm, out_dtype):
    M = xb.shape[0]
    nT = cos.shape[0] // tm
    return pl.pallas_call(
        _nsa_q_body, grid=(M // tm,),
        in_specs=[pl.BlockSpec((tm, D_MODEL), lambda i: (i, 0)),
                  pl.BlockSpec((D_MODEL, NSA_HEADS * LANES), lambda i: (0, 0)),
                  pl.BlockSpec((tm, LANES), lambda i: (i % nT, 0)),
                  pl.BlockSpec((tm, LANES), lambda i: (i % nT, 0))],
        out_specs=pl.BlockSpec((NSA_HEADS, tm, LANES), lambda i: (0, i, 0)),
        out_shape=jax.ShapeDtypeStruct((NSA_HEADS, M, LANES), out_dtype),
        compiler_params=_cp("parallel"), name="nsa_q")(xb, w_pad, cos, sin)


def _nsa_gate_body(x_ref, w_ref, o_ref):
    o_ref[0] = jax.nn.sigmoid(_dot(x_ref[...], w_ref[0]))


def _nsa_gate(xb, w_pad, tm):
    M = xb.shape[0]
    S = w_pad.shape[0]
    return pl.pallas_call(
        _nsa_gate_body, grid=(M // tm, S),
        in_specs=[pl.BlockSpec((tm, D_MODEL), lambda i, s: (i, 0)),
                  pl.BlockSpec((1, D_MODEL, LANES), lambda i, s: (s, 0, 0))],
        out_specs=pl.BlockSpec((1, tm, LANES), lambda i, s: (s, i, 0)),
        out_shape=jax.ShapeDtypeStruct((S, M, LANES), F32),
        compiler_params=_cp("parallel", "arbitrary"), name="nsa_gate")(xb, w_pad)


def _compress_rows(load, n_rows, w1_refs, w2_refs, pe_refs):
    lane = lax.broadcasted_iota(jnp.int32, (n_rows, LANES), 1)
    low = lane < NSA_HD
    out = []
    for kv in range(2):
        w1 = w1_refs[kv][...]
        bias = (_dot(pe_refs[kv][0], w1)[0:1, 0:CMP_HIDDEN]
                + _dot(pe_refs[kv][1], w1)[0:1, CMP_HIDDEN:2 * CMP_HIDDEN])
        g0, g1 = [], []
        for rp in range(CMP_STRIDE // 2):
            p = load(kv, 2 * rp)
            q = load(kv, 2 * rp + 1)
            g0.append(jnp.where(low, p, pltpu.roll(q, NSA_HD, 1)).astype(BF16))
            g1.append(jnp.where(low, pltpu.roll(p, NSA_HD, 1), q).astype(BF16))
        per_g = []
        for cols in (g0, g1):
            ab = _dot(jnp.concatenate(cols, axis=1), w1)
            nxt = pltpu.roll(ab[:, CMP_HIDDEN:2 * CMP_HIDDEN], n_rows - 1, 0)
            h = ab[:, 0:CMP_HIDDEN] + nxt + bias
            per_g.append(_dot(jax.nn.gelu(h).astype(BF16), w2_refs[kv][...]))
        out.append(per_g)
    return out


def _compress_prompt_body(x_ref, w1k, w1v, w2k, w2v, pek, pev, cos_ref, sin_ref, o_ref, *, n_rows):
    load = lambda kv, r: x_ref[:, r * 256 + kv * LANES:r * 256 + (kv + 1) * LANES]
    c = _compress_rows(load, n_rows, (w1k, w1v), (w2k, w2v), (pek, pev))
    for g in range(NSA_GROUPS):
        o_ref[0, g, 0] = _rope_lanes(c[0][g], cos_ref[...], sin_ref[...]).astype(BF16)
        o_ref[0, g, 1] = c[1][g].astype(BF16)


def _compress_prompt(xflat, cw, cos, sin, B, n_rows):
    body = functools.partial(_compress_prompt_body, n_rows=n_rows)
    const2 = lambda b: (0, 0)
    const3 = lambda b: (0, 0, 0)
    return pl.pallas_call(
        body, grid=(B,),
        in_specs=[pl.BlockSpec((n_rows, 4096), lambda b: (b, 0)),
                  pl.BlockSpec((1024, 256), const2), pl.BlockSpec((1024, 256), const2),
                  pl.BlockSpec((CMP_HIDDEN, LANES), const2), pl.BlockSpec((CMP_HIDDEN, LANES), const2),
                  pl.BlockSpec((2, 16, 1024), const3), pl.BlockSpec((2, 16, 1024), const3),
                  pl.BlockSpec((n_rows, LANES), const2), pl.BlockSpec((n_rows, LANES), const2)],
        out_specs=pl.BlockSpec((1, NSA_GROUPS, 2, n_rows, LANES), lambda b: (b, 0, 0, 0, 0)),
        out_shape=jax.ShapeDtypeStruct((B, NSA_GROUPS, 2, n_rows, LANES), BF16),
        compiler_params=_cp("parallel"), name="compress_prompt")(
            xflat, cw["w1k"], cw["w1v"], cw["w2k"], cw["w2v"], cw["pek"], cw["pev"], cos, sin)


def _compress_sample_body(pt_ref, *refs, n_rows, PG):
    pages = refs[:PG]
    w1k, w1v, w2k, w2v, pek, pev, cos_ref, sin_ref, o_ref, x_ref = refs[PG:]
    j = pl.program_id(1)
    cpp = PAGE_SIZE // CMP_STRIDE
    orow = lax.broadcasted_iota(jnp.int32, (PAGE_SIZE, PAGE_SIZE), 0)
    pcol = lax.broadcasted_iota(jnp.int32, (PAGE_SIZE, PAGE_SIZE), 1)
    perm = (pcol == _vmod(orow, cpp) * CMP_STRIDE + _vdiv(orow, cpp)).astype(BF16)
    for k in range(PG):
        c0 = pl.multiple_of((j * PG + k) * cpp, cpp)
        for kv in range(2):
            xp = _dot_nt(perm, pages[k][0, kv].astype(BF16))
            for r in range(CMP_STRIDE):
                x_ref[kv, r, pl.ds(c0, cpp), :] = xp[r * cpp:(r + 1) * cpp, :]

    @pl.when(j == pl.num_programs(1) - 1)
    def _():
        load = lambda kv, r: x_ref[kv, r]
        c = _compress_rows(load, n_rows, (w1k, w1v), (w2k, w2v), (pek, pev))
        ck = c[0][0] + pltpu.roll(c[0][1], NSA_HD, 1)
        cv = c[1][0] + pltpu.roll(c[1][1], NSA_HD, 1)
        o_ref[0, :, 0:128] = _rope_lanes(ck, cos_ref[...], sin_ref[...]).astype(BF16)
        o_ref[0, :, 128:256] = cv.astype(BF16)


def _compress_sample(cache_t, page_table, cw, cos, sin, PG):
    B, n_pages = page_table.shape
    n_rows = n_pages * PAGE_SIZE // CMP_STRIDE
    body = functools.partial(_compress_sample_body, n_rows=n_rows, PG=PG)
    const2 = lambda b, j, pt: (0, 0)
    const3 = lambda b, j, pt: (0, 0, 0)
    page_specs = [pl.BlockSpec((1, 2, LANES, PAGE_SIZE), functools.partial(
        lambda b, j, pt, k: (pt[b, j * PG + k], 0, 0, 0), k=k)) for k in range(PG)]
    grid_spec = pltpu.PrefetchScalarGridSpec(
        num_scalar_prefetch=1, grid=(B, n_pages // PG),
        in_specs=page_specs + [
            pl.BlockSpec((1024, 256), const2), pl.BlockSpec((1024, 256), const2),
            pl.BlockSpec((CMP_HIDDEN, LANES), const2), pl.BlockSpec((CMP_HIDDEN, LANES), const2),
            pl.BlockSpec((2, 16, 1024), const3), pl.BlockSpec((2, 16, 1024), const3),
            pl.BlockSpec((n_rows, LANES), const2), pl.BlockSpec((n_rows, LANES), const2)],
        out_specs=pl.BlockSpec((1, n_rows, 256), lambda b, j, pt: (b, 0, 0)),
        scratch_shapes=[pltpu.VMEM((2, CMP_STRIDE, n_rows, LANES), F32)])
    return pl.pallas_call(
        body, grid_spec=grid_spec,
        out_shape=jax.ShapeDtypeStruct((B, n_rows, 256), BF16),
        compiler_params=_cp("parallel", "arbitrary"), name="compress_sample")(
            page_table, *([cache_t] * PG), cw["w1k"], cw["w1v"], cw["w2k_lo"], cw["w2v_lo"],
            cw["pek"], cw["pev"], cos, sin)


def _nsa_attn_body(q_ref, gate_ref, ckv_ref, ovt_ref, kvx_ref, o_ref,
                   s_ref, p_ref, m_ref, l_ref, a_ref, acc_ref, *, Tq, Tk, n_cmp, n_sel):
    q0 = pl.program_id(2) * Tq
    R = NSA_HPG * Tq
    NB = 32
    q3 = q_ref[...]
    qh = q3.reshape(R, LANES)
    tq_lane = q0 + lax.broadcasted_iota(jnp.int32, (1, Tq), 1)
    t_lane = jnp.concatenate([tq_lane] * NSA_HPG, axis=1)

    ck = ckv_ref[0, 0, 0]
    cv = ckv_ref[0, 0, 1]
    n_rows = ck.shape[0]
    s = _dot_nt(ck, qh)
    nidx = lax.broadcasted_iota(jnp.int32, (n_rows, 1), 0)
    cmask = (nidx < n_cmp) & (CMP_STRIDE * nidx + (CMP_LEN - 1) <= t_lane)
    s = jnp.where(cmask, s, NEG_INF)
    e = jnp.where(cmask, jnp.exp(s - jnp.max(s, axis=0, keepdims=True)), 0.0)
    p = e / jnp.maximum(jnp.sum(e, axis=0, keepdims=True), 1e-30)
    o_cmp = _dot_tn(cv, p.astype(BF16))
    psum = p[:, 0:Tq]
    for h in range(1, NSA_HPG):
        psum = psum + p[:, h * Tq:(h + 1) * Tq]
    ph, plo = _split2(psum)
    imp = _dot(ovt_ref[...], ph) + _dot(ovt_ref[...], plo)

    sc = imp[NSA_HD:NSA_HD + NB, :]
    jj = lax.broadcasted_iota(jnp.int32, (NB, 1), 0)
    cur = _vdiv(tq_lane, SEL_BLOCK)
    forced = (jj == 0) | (jj == cur) | (jj == cur - 1)
    valid = SEL_BLOCK * jj <= tq_lane
    sc = jnp.where(valid, jnp.where(forced, FORCE_SCORE, sc), -FORCE_SCORE)
    rank = jnp.zeros((NB, Tq), F32)
    for jp in range(n_sel):
        c = sc[jp:jp + 1, :]
        beats = (c > sc) | ((c == sc) & (jp < jj))
        rank = rank + jnp.where(beats, 1.0, 0.0)
    drop = (rank >= float(min(SEL_TOP, n_sel))) & (jj < n_sel)
    bias_t = jnp.concatenate([jnp.zeros((NSA_HD, Tq), F32), jnp.where(drop, NEG_INF, 0.0),
                              jnp.zeros((LANES - NSA_HD - NB, Tq), F32)], axis=0)
    qa = (q3.astype(F32) + bias_t.T[None]).astype(BF16).reshape(R, LANES)

    def reset():
        m_ref[...] = jnp.full(m_ref.shape, NEG_INF, F32)
        l_ref[...] = jnp.zeros(l_ref.shape, F32)
        acc_ref[...] = jnp.zeros(acc_ref.shape, F32)

    def flash_tile(kt, qv, kcol, masked, window):
        k0 = pl.multiple_of(kt * Tk, Tk)
        ka = kvx_ref[0, pl.ds(k0, Tk), kcol:kcol + LANES]
        va = kvx_ref[0, pl.ds(k0, Tk), kcol + LANES:kcol + 2 * LANES]
        s_ref[...] = _dot_nt(ka, qv)
        if masked:
            kpos = k0 + lax.broadcasted_iota(jnp.int32, (Tk, 1), 0)
            ok = kpos <= tq_lane
            if window:
                ok = ok & (kpos > tq_lane - WINDOW)
        for h in range(NSA_HPG):
            cols = slice(h * Tq, (h + 1) * Tq)
            s = s_ref[:, cols]
            if masked:
                s = jnp.where(ok, s, NEG_INF)
            m_old = m_ref[0:1, cols]
            m_new = jnp.maximum(m_old, jnp.max(s, axis=0, keepdims=True))
            a = jnp.exp(m_old - m_new)
            p = jnp.exp(s - m_new)
            l_ref[0:1, cols] = a * l_ref[0:1, cols] + jnp.sum(p, axis=0, keepdims=True)
            m_ref[0:1, cols] = m_new
            a_ref[0:1, cols] = a
            p_ref[:, cols] = p.astype(BF16)
        acc_ref[...] = a_ref[0:1, :] * acc_ref[...] + _dot_tn(va, p_ref[...])

    def run(lo, hi, qv, kcol, masked, window):
        def step(kt, carry):
            flash_tile(kt, qv, kcol, masked, window)
            return carry
        lax.fori_loop(lo, hi, step, 0)

    kd = q0 // Tk
    reset()
    run(0, kd, qa, 0, False, False)
    flash_tile(kd, qa, 0, True, False)
    o_sel = acc_ref[...] / l_ref[0:1, :]

    reset()
    run(jnp.maximum(q0 - WINDOW + 1, 0) // Tk, kd + 1, qh, 2 * LANES, True, True)
    o_win = acc_ref[...] / l_ref[0:1, :]

    gates_t = gate_ref[0].T
    heads = []
    for h in range(NSA_HPG):
        cols = slice(h * Tq, (h + 1) * Tq)
        o_t = (gates_t[h:h + 1, :] * o_cmp[:, cols]
               + gates_t[NSA_HPG + h:NSA_HPG + h + 1, :] * o_sel[:, cols]
               + gates_t[2 * NSA_HPG + h:2 * NSA_HPG + h + 1, :] * o_win[:, cols])
        heads.append(o_t.T)
    low = lax.broadcasted_iota(jnp.int32, (1, LANES), 1) < NSA_HD
    for hp in range(NSA_HPG // 2):
        pair = jnp.where(low, heads[2 * hp], pltpu.roll(heads[2 * hp + 1], NSA_HD, 1))
        o_ref[:, hp * LANES:(hp + 1) * LANES] = pair.astype(o_ref.dtype)


def _nsa_attn_prompt(q, gates, ckv, ov, kvx, B, T, Tq, Tk, n_cmp, n_sel):
    nq = T // Tq
    M = B * T
    n_rows = ckv.shape[3]
    assert Tk % Tq == 0 and Tq % LANES == 0 and n_sel <= 32
    R = NSA_HPG * Tq
    body = functools.partial(_nsa_attn_body, Tq=Tq, Tk=Tk, n_cmp=n_cmp, n_sel=n_sel)
    scratch = [pltpu.VMEM((Tk, R), F32), pltpu.VMEM((Tk, R), BF16), pltpu.VMEM((8, R), F32),
               pltpu.VMEM((8, R), F32), pltpu.VMEM((8, R), F32), pltpu.VMEM((LANES, R), F32)]
    return pl.pallas_call(
        body, grid=(B, NSA_GROUPS, nq),
        in_specs=[pl.BlockSpec((NSA_HPG, Tq, LANES), lambda b, g, i: (g, b * nq + i, 0)),
                  pl.BlockSpec((1, Tq, LANES), lambda b, g, i: (g, b * nq + i, 0)),
                  pl.BlockSpec((1, 1, 2, n_rows, LANES), lambda b, g, i: (b, g, 0, 0, 0)),
                  pl.BlockSpec((LANES, n_rows), lambda b, g, i: (0, 0)),
                  pl.BlockSpec((1, T, 512), lambda b, g, i: (g, b, 0))],
        out_specs=pl.BlockSpec((Tq, NSA_HPG * NSA_HD), lambda b, g, i: (b * nq + i, g)),
        out_shape=jax.ShapeDtypeStruct((M, NSA_HEADS * NSA_HD), BF16),
        scratch_shapes=scratch,
        compiler_params=_cp("parallel", "parallel", "arbitrary"), name="nsa_attn_prompt")(
            q, gates, ckv, ov, kvx)


def _col_softmax_update(s, m, l, acc, v):
    m_new = jnp.maximum(m, jnp.max(s, axis=0, keepdims=True))
    a = jnp.exp(m - m_new)
    p = jnp.exp(s - m_new)
    l = a * l + jnp.sum(p, axis=0, keepdims=True)
    return m_new, l, a, _dot_tn(p.astype(BF16), v)


def _nsa_attn_sample_body(pt_ref, *refs, T, PG, past_len, n_cmp, n_sel, n_rows, wb):
    pages = refs[:PG]
    (q_ref, gate_ref, ckv_ref, ovt_ref, swin_ref, wnew_ref, snew_ref, o_ref,
     qb_ref, selb_ref, m_ref, l_ref, acc_ref, oc_ref, ow_ref) = refs[PG:]
    j = pl.program_id(1)
    NQ = NSA_HEADS * T
    lane = lax.broadcasted_iota(jnp.int32, (1, NQ), 1)
    tl = past_len + _vmod(lane, T)
    eye = (lax.broadcasted_iota(jnp.int32, (NQ, NQ), 0) == lax.broadcasted_iota(jnp.int32, (NQ, NQ), 1))

    def to_rows(x):
        return jnp.sum(jnp.where(eye, x, 0.0), axis=1, keepdims=True)

    @pl.when(j == 0)
    def _():
        qp = q_ref[...].reshape(NQ, LANES)
        rowi = lax.broadcasted_iota(jnp.int32, (NQ, LANES), 0)
        qb = jnp.where(rowi < NQ // 2, qp, pltpu.roll(qp, NSA_HD, 1)).astype(BF16)
        qb_ref[...] = qb

        ck = ckv_ref[0, :, 0:128]
        cv = ckv_ref[0, :, 128:256]
        s = _dot_nt(ck, qb)
        nidx = lax.broadcasted_iota(jnp.int32, (n_rows, 1), 0)
        cmask = (nidx < n_cmp) & (CMP_STRIDE * nidx + (CMP_LEN - 1) <= tl)
        s = jnp.where(cmask, s, NEG_INF)
        e = jnp.where(cmask, jnp.exp(s - jnp.max(s, axis=0, keepdims=True)), 0.0)
        p = e / jnp.maximum(jnp.sum(e, axis=0, keepdims=True), 1e-30)
        oc_ref[...] = _dot_tn(p.astype(BF16), cv)
        ri = lax.broadcasted_iota(jnp.int32, (NQ, NQ), 0)
        ci = lax.broadcasted_iota(jnp.int32, (NQ, NQ), 1)
        same = ((_vdiv(ri, NQ // 2) == _vdiv(ci, NQ // 2)) & (_vmod(ri, T) == _vmod(ci, T))).astype(BF16)
        ph, plo = _split2(p)
        psum = _dot(ph, same) + _dot(plo, same)
        sh, slo = _split2(psum)
        imp = _dot(ovt_ref[...], sh) + _dot(ovt_ref[...], slo)

        nb = imp.shape[0]
        jj = lax.broadcasted_iota(jnp.int32, (nb, 1), 0)
        cur = _vdiv(tl, SEL_BLOCK)
        forced = (jj == 0) | (jj == cur) | (jj == cur - 1)
        valid = SEL_BLOCK * jj <= tl
        score = jnp.where(valid, jnp.where(forced, FORCE_SCORE, imp), -FORCE_SCORE)
        work = jnp.where(jj < n_sel, score, -2e38)
        keep = jnp.zeros(work.shape, jnp.bool_)
        for _ in range(min(SEL_TOP, n_sel)):
            mx = jnp.max(work, axis=0, keepdims=True)
            first = jnp.min(jnp.where(work == mx, jj, nb), axis=0, keepdims=True)
            pick = jj == first
            keep = keep | pick
            work = jnp.where(pick, -3e38, work)
        selb_ref[...] = jnp.where(keep, 0.0, NEG_INF)

        kw = swin_ref[0, 0].T.astype(BF16)
        vw = swin_ref[0, 1].T.astype(BF16)
        sw = _dot_nt(kw, qb)
        kpos = past_len - wb + lax.broadcasted_iota(jnp.int32, (wb, 1), 0)
        sw = jnp.where((kpos <= tl) & (kpos > tl - WINDOW), sw, NEG_INF)
        zero = jnp.zeros((T, 256), F32)
        wn = jnp.concatenate([wnew_ref[...], zero], axis=0)
        sn = _dot_nt(wn[:, 0:128].astype(BF16), qb)
        kposn = past_len + lax.broadcasted_iota(jnp.int32, (2 * T, 1), 0)
        sn = jnp.where((kposn <= tl) & (kposn > tl - WINDOW), sn, NEG_INF)
        mw = jnp.maximum(jnp.max(sw, axis=0, keepdims=True), jnp.max(sn, axis=0, keepdims=True))
        pw = jnp.exp(sw - mw)
        pn = jnp.exp(sn - mw)
        lw = jnp.sum(pw, axis=0, keepdims=True) + jnp.sum(pn, axis=0, keepdims=True)
        ow = _dot_tn(pw.astype(BF16), vw) + _dot_tn(pn.astype(BF16), wn[:, 128:256].astype(BF16))
        ow_ref[...] = ow / to_rows(lw)

        m_ref[...] = jnp.full(m_ref.shape, NEG_INF, F32)
        l_ref[...] = jnp.zeros(l_ref.shape, F32)
        acc_ref[...] = jnp.zeros(acc_ref.shape, F32)

    qb = qb_ref[...]
    eye_b = (lax.broadcasted_iota(jnp.int32, (PAGE_SIZE, PAGE_SIZE), 0)
             == lax.broadcasted_iota(jnp.int32, (PAGE_SIZE, PAGE_SIZE), 1)).astype(BF16)
    flip = lambda a: _dot_nt(eye_b, a.astype(BF16)).astype(BF16)
    kt = jnp.concatenate([flip(pg[0, 0]) for pg in pages], axis=0)
    vt = jnp.concatenate([flip(pg[0, 1]) for pg in pages], axis=0)
    blk_per_step = PG * PAGE_SIZE // SEL_BLOCK
    sb = selb_ref[pl.ds(pl.multiple_of(j * blk_per_step, blk_per_step), blk_per_step), :]
    sb = jnp.broadcast_to(sb[:, None, :], (blk_per_step, SEL_BLOCK, NQ)).reshape(PG * PAGE_SIZE, NQ)
    sc = _dot_nt(kt, qb) + sb
    m_new, l_new, a, pv = _col_softmax_update(sc, m_ref[0:1, :], l_ref[0:1, :], None, vt)
    m_ref[0:1, :] = m_new
    l_ref[0:1, :] = l_new
    acc_ref[...] = to_rows(a) * acc_ref[...] + pv

    @pl.when(j == pl.num_programs(1) - 1)
    def _():
        zero = jnp.zeros((T, 256), F32)
        sn_rows = jnp.concatenate([snew_ref[...], zero], axis=0)
        kposn = past_len + lax.broadcasted_iota(jnp.int32, (2 * T, 1), 0)
        n_past_blk = past_len // SEL_BLOCK
        sbn = selb_ref[n_past_blk:n_past_blk + 1, :]
        sc = _dot_nt(sn_rows[:, 0:128].astype(BF16), qb) + sbn
        sc = jnp.where(kposn <= tl, sc, NEG_INF)
        m_new, l_new, a, pv = _col_softmax_update(sc, m_ref[0:1, :], l_ref[0:1, :], None,
                                                  sn_rows[:, 128:256].astype(BF16))
        o_sel = (to_rows(a) * acc_ref[...] + pv) / to_rows(l_new)
        gates = gate_ref[...].reshape(NQ, LANES)
        o = gates[:, 0:1] * oc_ref[...] + gates[:, 1:2] * o_sel + gates[:, 2:3] * ow_ref[...]
        low = lax.broadcasted_iota(jnp.int32, (T, LANES), 1) < NSA_HD
        for hp in range(NSA_HEADS // 2):
            a0 = o[(2 * hp) * T:(2 * hp + 1) * T]
            a1 = o[(2 * hp + 1) * T:(2 * hp + 2) * T]
            if 2 * hp < NSA_HPG:
                piece = jnp.where(low, a0, pltpu.roll(a1, NSA_HD, 1))
            else:
                piece = jnp.where(low, pltpu.roll(a0, NSA_HD, 1), a1)
            o_ref[:, hp * LANES:(hp + 1) * LANES] = piece


def _nsa_attn_sample(q, gates, ckv, ovt, cache_sel_t, page_table, state_win_t, win_new, sel_new, T, PG):
    B, n_pages = page_table.shape
    past_len = n_pages * PAGE_SIZE
    n_rows = ckv.shape[1]
    wb = state_win_t.shape[3]
    n_cmp = (past_len + T) // CMP_STRIDE - 1
    n_sel = past_len // SEL_BLOCK + -(-T // SEL_BLOCK)
    nb = ovt.shape[0]
    NQ = NSA_HEADS * T
    body = functools.partial(_nsa_attn_sample_body, T=T, PG=PG, past_len=past_len, n_cmp=n_cmp,
                             n_sel=n_sel, n_rows=n_rows, wb=wb)
    page_specs = [pl.BlockSpec((1, 2, LANES, PAGE_SIZE), functools.partial(
        lambda b, j, pt, k: (pt[b, j * PG + k], 0, 0, 0), k=k)) for k in range(PG)]
    grid_spec = pltpu.PrefetchScalarGridSpec(
        num_scalar_prefetch=1, grid=(B, n_pages // PG),
        in_specs=page_specs + [
            pl.BlockSpec((NSA_HEADS, T, LANES), lambda b, j, pt: (0, b, 0)),
            pl.BlockSpec((NSA_HEADS, T, LANES), lambda b, j, pt: (0, b, 0)),
            pl.BlockSpec((1, n_rows, 256), lambda b, j, pt: (b, 0, 0)),
            pl.BlockSpec((nb, n_rows), lambda b, j, pt: (0, 0)),
            pl.BlockSpec((1, 2, LANES, wb), lambda b, j, pt: (b, 0, 0, 0)),
            pl.BlockSpec((T, 256), lambda b, j, pt: (b, 0)),
            pl.BlockSpec((T, 256), lambda b, j, pt: (b, 0))],
        out_specs=pl.BlockSpec((T, NSA_HEADS * NSA_HD), lambda b, j, pt: (b, 0)),
        scratch_shapes=[pltpu.VMEM((NQ, LANES), BF16), pltpu.VMEM((nb, NQ), F32),
                        pltpu.VMEM((8, NQ), F32), pltpu.VMEM((8, NQ), F32),
                        pltpu.VMEM((NQ, LANES), F32), pltpu.VMEM((NQ, LANES), F32),
                        pltpu.VMEM((NQ, LANES), F32)])
    return pl.pallas_call(
        body, grid_spec=grid_spec,
        out_shape=jax.ShapeDtypeStruct((B * T, NSA_HEADS * NSA_HD), F32),
        compiler_params=_cp("parallel", "arbitrary"), name="nsa_attn_sample")(
            page_table, *([cache_sel_t] * PG), q, gates, ckv, ovt, state_win_t, win_new, sel_new)


def _rope_tables(pos, pattern):
    half = NSA_HD // 2
    inv = ROPE_THETA ** (-jnp.arange(half, dtype=F32) / half)
    ang = pos.astype(F32)[:, None] * inv[None, :]
    cos, sin = jnp.cos(ang), jnp.sin(ang)
    zero = jnp.zeros_like(cos)
    cs = [jnp.concatenate([cos, cos], 1) if p else jnp.concatenate([zero, zero], 1) for p in pattern]
    sn = [jnp.concatenate([-sin, sin], 1) if p else jnp.concatenate([zero, zero], 1) for p in pattern]
    return jnp.concatenate(cs, 1), jnp.concatenate(sn, 1)


def _overlap(n_rows, n_sel):
    ci = jnp.arange(n_rows, dtype=jnp.int32)[:, None] * CMP_STRIDE
    sj = jnp.arange(n_sel, dtype=jnp.int32)[None, :] * SEL_BLOCK
    return ((ci < sj + SEL_BLOCK) & (ci + CMP_LEN > sj)).astype(F32)


def _prep_weights(w):
    d = D_MODEL
    hk, hv = GLA_HEADS * GLA_DK, GLA_HEADS * GLA_DV
    p = {}
    p["ln_g"] = w["ln_g"].reshape(DEPTH, 2, 1, d)
    p["ln_b"] = w["ln_b"].reshape(DEPTH, 2, 1, d)
    w_in = w["gla_w_in"][0]
    p["gla_w_qkvr"] = w_in[:, :2 * hk + 2 * hv].astype(BF16)
    p["gla_w_a"] = jnp.pad(w_in[:, 2 * hk + 2 * hv:], ((0, 0), (0, LANES - GLA_GATE_RANK))).astype(BF16)
    p["gla_w_a2"] = jnp.pad(w["gla_w_a2"][0], ((0, LANES - GLA_GATE_RANK), (0, 0))).astype(BF16)
    p["gla_b_a"] = w["gla_b_a"][0].reshape(1, hk)
    p["gla_norm_g"] = w["gla_norm_g"][0].reshape(1, GLA_DV)
    p["gla_w_o"] = w["gla_w_o"][0].astype(BF16)
    p["nsa_w_kv"] = w["nsa_w_kv"].astype(BF16)
    nw = w["nsa_w_in"][0]
    wq = nw[:, :NSA_HEADS * NSA_HD].reshape(d, NSA_HEADS, NSA_HD)
    p["nsa_w_q"] = jnp.pad(wq, ((0, 0), (0, 0), (0, LANES - NSA_HD))).reshape(d, NSA_HEADS * LANES).astype(BF16)
    wg = nw[:, NSA_HEADS * NSA_HD:].reshape(d, 3, NSA_GROUPS, NSA_HPG)
    wg_grp = jnp.transpose(wg, (2, 0, 1, 3)).reshape(NSA_GROUPS, d, 3 * NSA_HPG)
    p["nsa_w_gate_grp"] = jnp.pad(wg_grp, ((0, 0), (0, 0), (0, LANES - 3 * NSA_HPG))).astype(BF16)
    wg_head = jnp.transpose(wg, (2, 3, 0, 1)).reshape(NSA_HEADS, d, 3)
    p["nsa_w_gate_head"] = jnp.pad(wg_head, ((0, 0), (0, 0), (0, LANES - 3))).astype(BF16)
    p["nsa_w_o"] = w["nsa_w_o"][0].astype(BF16)
    cw = {}
    for name, pe, w1, w2 in (("k", w["cmp_pe_k"], w["cmp_w1_k"], w["cmp_w2_k"]),
                             ("v", w["cmp_pe_v"], w["cmp_w1_v"], w["cmp_w2_v"])):
        half = CMP_STRIDE * NSA_HD
        cw["w1" + name] = jnp.concatenate([w1[:half], w1[half:]], axis=1).astype(BF16)
        cw["w2" + name] = jnp.pad(w2, ((0, 0), (0, LANES - NSA_HD))).astype(BF16)
        cw["w2" + name + "_lo"] = cw["w2" + name]
        pe2 = pe.reshape(2, 1, half)
        cw["pe" + name] = jnp.broadcast_to(pe2, (2, 16, half)).astype(BF16)
    p["cmp"] = cw
    p["ffn_w_gate"] = w["ffn_w_gate"][0].astype(BF16)
    p["ffn_w_up"] = w["ffn_w_up"][0].astype(BF16)
    p["ffn_w_down"] = w["ffn_w_down"][0].astype(BF16)
    p["moe_w_router"] = jnp.pad(w["moe_w_router"][0], ((0, 0), (0, LANES - N_EXPERTS)))
    p["moe_w_gate"] = w["moe_w_gate"][0].astype(BF16)
    p["moe_w_up"] = w["moe_w_up"][0].astype(BF16)
    p["moe_w_down"] = w["moe_w_down"][0].astype(BF16)
    p["ple_w_proj"] = w["ple_w_proj"].astype(BF16)
    p["ple_w_gate"] = w["ple_w_gate"].astype(BF16)
    return p


def _tile(M, cap):
    t = cap
    while M % t:
        t //= 2
    return t


def _trunk(x, ple, pos0, gla_s0, past, p):
    B, T, d = x.shape
    M = B * T
    tm = _tile(M, 1024)
    tm_s = _tile(M, 512)
    xf = x.reshape(M, d)
    xb = xf.astype(BF16)
    pleb = ple.reshape(DEPTH, M, PLE_DIM).astype(BF16)

    proj = _proj(xb, p["gla_w_qkvr"], tm, 512, F32)
    la = _gla_gate(xb, p["gla_w_a"], p["gla_w_a2"], p["gla_b_a"], tm)
    if T >= GLA_CHUNK:
        C, NC, Tp = GLA_CHUNK, _tile(T // GLA_CHUNK, 4), T
    else:
        C, NC, Tp = 16, 1, 16
        pad = lambda a: jnp.pad(a.reshape(B, T, -1), ((0, 0), (0, Tp - T), (0, 0))).reshape(B * Tp, -1)
        proj, la = pad(proj), pad(la)
    o, s_new = _gla(proj, la, p["gla_norm_g"], gla_s0, B, Tp, C, NC)
    if Tp != T:
        o = o.reshape(B, Tp, -1)[:, :T].reshape(M, -1)
    x1f, x1b = _mm_ln(o, p["gla_w_o"], xf, p["ln_g"][0, 0], p["ln_b"][0, 0], tm_s)
    x2f, x2b = _ffn(x1b, x1f, p["ffn_w_gate"], p["ffn_w_up"], p["ffn_w_down"],
                    p["ln_g"][0, 1], p["ln_b"][0, 1], tm, 256)
    x3f, x3b = _ple(x2b, x2f, pleb[0], p["ple_w_gate"][0], p["ple_w_proj"][0], tm_s)

    qpos = pos0 + jnp.arange(T, dtype=jnp.int32)
    cos_kv, sin_kv = _rope_tables(qpos, (1, 1))
    cos_q, sin_q = _rope_tables(qpos, (1, 0))
    if past is None:
        tq = _tile(T, 512)
    else:
        tq = _tile(M, 512)
        cos_kv, sin_kv, cos_q, sin_q = (jnp.tile(a, (B, 1)) for a in (cos_kv, sin_kv, cos_q, sin_q))
    kv = _nsa_kv(x3b, p["nsa_w_kv"], cos_kv, sin_kv, tq, past is None)
    cmp_rows = kv[0]
    from_t = lambda a: jnp.transpose(a.reshape(B, 2, NSA_GROUPS, NSA_HD, a.shape[-1]), (0, 4, 1, 2, 3))
    to_t = lambda a: jnp.transpose(a, (0, 2, 3, 4, 1)).reshape(a.shape[0], 2, NSA_GROUPS * NSA_HD, a.shape[1])
    shp = (B, T, 2, NSA_GROUPS, NSA_HD)
    if past is None:
        n_ch = T // CMP_STRIDE
        n_cmp = n_ch - 1
        n_sel = T // SEL_BLOCK
        cpos = jnp.arange(n_ch, dtype=jnp.int32) * CMP_STRIDE
        cos_c, sin_c = _rope_tables(cpos, (1, 0))
        ckv = _compress_prompt(cmp_rows.reshape(B * n_ch, CMP_STRIDE * 256), p["cmp"], cos_c, sin_c, B, n_ch)
        ov = jnp.pad(_overlap(n_ch, n_sel).T, ((NSA_HD, LANES - NSA_HD - n_sel), (0, 0))).astype(BF16)
        q = _nsa_q(x3b, p["nsa_w_q"], cos_q, sin_q, tq, BF16)
        gates = _nsa_gate(x3b, p["nsa_w_gate_grp"], tm_s)
        o = _nsa_attn_prompt(q, gates, ckv, ov, kv[1], B, T, _tile(T, 128), _tile(T, 512), n_cmp, n_sel)
        cmp_out, sel_out = from_t(kv[2]), from_t(kv[3])
        win_out = from_t(kv[4][..., T - min(WINDOW, T):])
    else:
        cache_cmp, cache_sel, page_table, state_win = past
        n_pool = cache_cmp.shape[0]
        n_pages = page_table.shape[1]
        past_len = n_pages * PAGE_SIZE
        PG = _tile(n_pages, 16)
        n_rows = past_len // CMP_STRIDE
        n_sel = past_len // SEL_BLOCK + -(-T // SEL_BLOCK)
        cpos = jnp.arange(n_rows, dtype=jnp.int32) * CMP_STRIDE
        cos_c, sin_c = _rope_tables(cpos, (1, 1))
        sel_rows, win_rows = kv[1], kv[2]
        ckv = _compress_sample(to_t(cache_cmp), page_table, p["cmp"], cos_c, sin_c, PG)
        nb = -(-n_sel // LANES) * LANES
        ovt = jnp.pad(_overlap(n_rows, n_sel).T, ((0, nb - n_sel), (0, 0))).astype(BF16)
        assert T < CMP_STRIDE and T <= SEL_BLOCK
        q = _nsa_q(x3b, p["nsa_w_q"], cos_q, sin_q, tq, F32)
        gates = _nsa_gate(x3b, p["nsa_w_gate_head"], tm_s)
        state_win_t = to_t(state_win)
        o = _nsa_attn_sample(q, gates, ckv, ovt, to_t(cache_sel), page_table,
                             state_win_t, win_rows, sel_rows, T, PG).astype(BF16)
        cmp_out, sel_out = cmp_rows.reshape(shp), sel_rows.reshape(shp)
        win_out = from_t(jnp.concatenate([state_win_t, to_t(win_rows.reshape(shp))], axis=-1)[..., T:])
    x4f, x4b = _mm_ln(o, p["nsa_w_o"], x3f, p["ln_g"][1, 0], p["ln_b"][1, 0], tm_s)
    tb = _tile(M, 1024)
    nb, LT, n_tiles, n_groups = _moe_sizes(M, tb)
    info, info_t, cnt = _router(x4f, p["moe_w_router"], tb)
    plan = _moe_plan(cnt[:, 0, :N_EXPERTS].astype(jnp.int32), LT, n_tiles, n_groups)
    xs = _moe_dispatch(x4b, info_t, plan, tb)
    ys = _moe_ffn(xs, plan, p["moe_w_gate"], p["moe_w_up"], p["moe_w_down"], n_groups, 896)
    x5f, x5b = _moe_combine(ys, info, x4f, plan, p["ln_g"][1, 1], p["ln_b"][1, 1], tb)
    x6f, _ = _ple(x5b, x5f, pleb[1], p["ple_w_gate"][1], p["ple_w_proj"][1], tm_s)

    return (x6f.reshape(B, T, d), s_new[None], cmp_out, sel_out, win_out)


def kernel(x_prompt, x_sample, state_gla, cache_cmp, cache_sel, state_win, page_table, p_prompt, p_sample,
           ln_g, ln_b, gla_w_in, gla_w_a2, gla_b_a, gla_norm_g, gla_w_o,
           nsa_w_kv, cmp_pe_k, cmp_w1_k, cmp_w2_k, cmp_pe_v, cmp_w1_v, cmp_w2_v,
           nsa_w_in, nsa_w_o, ffn_w_gate, ffn_w_up, ffn_w_down,
           moe_w_router, moe_w_gate, moe_w_up, moe_w_down, ple_w_proj, ple_w_gate):
    w = dict(ln_g=ln_g, ln_b=ln_b, gla_w_in=gla_w_in, gla_w_a2=gla_w_a2, gla_b_a=gla_b_a,
             gla_norm_g=gla_norm_g, gla_w_o=gla_w_o, nsa_w_kv=nsa_w_kv,
             cmp_pe_k=cmp_pe_k, cmp_w1_k=cmp_w1_k, cmp_w2_k=cmp_w2_k,
             cmp_pe_v=cmp_pe_v, cmp_w1_v=cmp_w1_v, cmp_w2_v=cmp_w2_v,
             nsa_w_in=nsa_w_in, nsa_w_o=nsa_w_o, ffn_w_gate=ffn_w_gate, ffn_w_up=ffn_w_up,
             ffn_w_down=ffn_w_down, moe_w_router=moe_w_router, moe_w_gate=moe_w_gate,
             moe_w_up=moe_w_up, moe_w_down=moe_w_down, ple_w_proj=ple_w_proj, ple_w_gate=ple_w_gate)
    p = _prep_weights(w)
    Bp = x_prompt.shape[0]
    gla_zero = jnp.zeros((Bp, GLA_HEADS, GLA_DK, GLA_DV), F32)
    yp, gp, cp, sp, wp = _trunk(x_prompt, p_prompt, 0, gla_zero, None, p)
    past_len = page_table.shape[1] * PAGE_SIZE
    ys, gs, cs, ss, ws = _trunk(x_sample, p_sample, past_len, state_gla[0],
                                (cache_cmp, cache_sel, page_table, state_win), p)
    return (yp, ys, gp, gs, cp, cs, sp, ss, wp, ws)
```

```python
import functools
import math

import jax
import jax.numpy as jnp
from jax import lax
from jax.experimental import pallas as pl
from jax.experimental.pallas import tpu as pltpu

F32 = jnp.float32
BF16 = jnp.bfloat16

D_MODEL = 1024
DEPTH = 2
PAGE_SIZE = 128
PLE_DIM = 256
DEEPNORM_ALPHA = (2.0 * DEPTH) ** 0.25
LN_EPS = 1e-5
ROPE_THETA = 10000.0

GLA_HEADS = 4
GLA_DK = 128
GLA_DV = 256
GLA_GATE_RANK = 16
GLA_TAU = 16.0
GLA_CHUNK = 64

NSA_HEADS = 16
NSA_HD = 64
NSA_GROUPS = 2
NSA_HPG = 8
CMP_STRIDE = 16
CMP_LEN = 32
CMP_HIDDEN = 128
SEL_BLOCK = 64
SEL_TOP = 16
WINDOW = 512
NEG_INF = -1e30
FORCE_SCORE = 1e9

N_EXPERTS = 8
LANES = 128
VMEM_LIMIT = 56 * 1024 * 1024

_NT = (((1,), (1,)), ((), ()))
_TN = (((0,), (0,)), ((), ()))


def _cp(*sem):
    return pltpu.CompilerParams(dimension_semantics=sem, vmem_limit_bytes=VMEM_LIMIT)


def _dot(a, b):
    return jnp.dot(a, b, preferred_element_type=F32)


def _dot_nt(a, b):
    return lax.dot_general(a, b, _NT, preferred_element_type=F32)


def _dot_tn(a, b):
    return lax.dot_general(a, b, _TN, preferred_element_type=F32)


def _split2(x):
    hi = x.astype(BF16)
    lo = (x - hi.astype(F32)).astype(BF16)
    return hi, lo


def _split3(x):
    hi = x.astype(BF16)
    r = x - hi.astype(F32)
    mid = r.astype(BF16)
    lo = (r - mid.astype(F32)).astype(BF16)
    return hi, mid, lo


def _vmod(x, n):
    assert n & (n - 1) == 0
    return x & (n - 1)


def _vdiv(x, n):
    assert n & (n - 1) == 0
    return x >> (n.bit_length() - 1)


def _layer_norm(y, g, b):
    mu = jnp.mean(y, axis=-1, keepdims=True)
    yc = y - mu
    var = jnp.mean(yc * yc, axis=-1, keepdims=True)
    return yc * lax.rsqrt(var + LN_EPS) * g + b


def _rope_lanes(x, cos, sin):
    lane = lax.broadcasted_iota(jnp.int32, x.shape, x.ndim - 1)
    first = _vmod(lane, NSA_HD) < (NSA_HD // 2)
    partner = jnp.where(first, pltpu.roll(x, LANES - NSA_HD // 2, x.ndim - 1),
                        pltpu.roll(x, NSA_HD // 2, x.ndim - 1))
    return x * cos + partner * sin


def _proj_body(x_ref, w_ref, o_ref):
    o_ref[...] = _dot(x_ref[...], w_ref[...]).astype(o_ref.dtype)


def _proj(x, w, tm, tn, out_dtype):
    M, K = x.shape
    N = w.shape[1]
    return pl.pallas_call(
        _proj_body, grid=(M // tm, N // tn),
        in_specs=[pl.BlockSpec((tm, K), lambda i, j: (i, 0)),
                  pl.BlockSpec((K, tn), lambda i, j: (0, j))],
        out_specs=pl.BlockSpec((tm, tn), lambda i, j: (i, j)),
        out_shape=jax.ShapeDtypeStruct((M, N), out_dtype),
        compiler_params=_cp("parallel", "arbitrary"), name="proj")(x, w)


def _gla_gate_body(x_ref, wa_ref, wa2_ref, ba_ref, o_ref):
    a = _dot(x_ref[...], wa_ref[...])
    z = _dot(a.astype(BF16), wa2_ref[...]) + ba_ref[...]
    o_ref[...] = (jnp.minimum(z, 0.0) - jnp.log1p(jnp.exp(-jnp.abs(z)))) * (1.0 / GLA_TAU)


def _gla_gate(xb, wa, wa2, ba, tm):
    M = xb.shape[0]
    N = wa2.shape[1]
    return pl.pallas_call(
        _gla_gate_body, grid=(M // tm,),
        in_specs=[pl.BlockSpec((tm, D_MODEL), lambda i: (i, 0)),
                  pl.BlockSpec((D_MODEL, LANES), lambda i: (0, 0)),
                  pl.BlockSpec((LANES, N), lambda i: (0, 0)),
                  pl.BlockSpec((1, N), lambda i: (0, 0))],
        out_specs=pl.BlockSpec((tm, N), lambda i: (i, 0)),
        out_shape=jax.ShapeDtypeStruct((M, N), F32),
        compiler_params=_cp("parallel"), name="gla_gate")(xb, wa, wa2, ba)


def _gla_body(q_ref, k_ref, v_ref, r_ref, la_ref, g_ref, s0_ref, o_ref, sout_ref, s_ref, *, C, NC):
    t = pl.program_id(1)

    @pl.when(t == 0)
    def _():
        s_ref[...] = s0_ref[0]

    row = lax.broadcasted_iota(jnp.int32, (C, C), 0)
    col = lax.broadcasted_iota(jnp.int32, (C, C), 1)
    tri = col <= row
    ones = jnp.ones((C, LANES), BF16)
    rowk = lax.broadcasted_iota(jnp.int32, (C, GLA_DK), 0)
    rt = lax.broadcasted_iota(jnp.int32, (C * NC, C * NC), 0)
    ct = lax.broadcasted_iota(jnp.int32, (C * NC, C * NC), 1)
    tri_bd = ((ct <= rt) & (_vdiv(rt, C) == _vdiv(ct, C))).astype(BF16)
    la_hi, la_mid, la_lo = _split3(la_ref[...])
    b_all = _dot(tri_bd, la_hi) + _dot(tri_bd, la_mid) + _dot(tri_bd, la_lo)
    for h in range(GLA_HEADS):
        kc = slice(h * GLA_DK, (h + 1) * GLA_DK)
        vc = slice(h * GLA_DV, (h + 1) * GLA_DV)
        state = s_ref[h]
        for c in range(NC):
            sl = pl.ds(c * C, C)
            rs = slice(c * C, (c + 1) * C)
            b = b_all[rs, kc]
            tot = (_dot_tn(la_hi[rs, kc], ones) + _dot_tn(la_mid[rs, kc], ones)
                   + _dot_tn(la_lo[rs, kc], ones))
            b_mid = jnp.sum(jnp.where(rowk == (C - 1) // 2, b, 0.0), axis=0, keepdims=True)
            b_last = jnp.sum(jnp.where(rowk == C - 1, b, 0.0), axis=0, keepdims=True)
            q = q_ref[sl, kc] * (GLA_DK ** -0.5)
            k = k_ref[sl, kc]
            v = v_ref[sl, vc].astype(BF16)
            qe = (q * jnp.exp(b - b_mid)).astype(BF16)
            ke = (k * jnp.exp(b_mid - b)).astype(BF16)
            a = jnp.where(tri, _dot_nt(qe, ke), 0.0)
            o = _dot(a.astype(BF16), v) + _dot((q * jnp.exp(b)).astype(BF16), state.astype(BF16))
            kd = (k * jnp.exp(b_last - b)).astype(BF16)
            state = jnp.exp(tot[:, 0:1]) * state + _dot_tn(kd, v)
            ms = jnp.mean(o * o, axis=-1, keepdims=True)
            rr = r_ref[sl, vc]
            o = o * lax.rsqrt(ms + LN_EPS) * g_ref[...] * (rr * jax.nn.sigmoid(rr))
            o_ref[sl, vc] = o.astype(o_ref.dtype)
        s_ref[h] = state

    @pl.when(t == pl.num_programs(1) - 1)
    def _():
        sout_ref[0] = s_ref[...]


def _gla(proj, la, norm_g, s0, B, T, C, NC):
    Tt = C * NC
    nT = T // Tt
    M = B * T
    hk, hv = GLA_HEADS * GLA_DK, GLA_HEADS * GLA_DV
    body = functools.partial(_gla_body, C=C, NC=NC)
    state_spec = pl.BlockSpec((1, GLA_HEADS, GLA_DK, GLA_DV), lambda b, t: (b, 0, 0, 0))
    return pl.pallas_call(
        body, grid=(B, nT),
        in_specs=[pl.BlockSpec((Tt, hk), lambda b, t: (b * nT + t, 0)),
                  pl.BlockSpec((Tt, hk), lambda b, t: (b * nT + t, 1)),
                  pl.BlockSpec((Tt, hv), lambda b, t: (b * nT + t, 2 * hk // hv)),
                  pl.BlockSpec((Tt, hv), lambda b, t: (b * nT + t, 2 * hk // hv + 1)),
                  pl.BlockSpec((Tt, hk), lambda b, t: (b * nT + t, 0)),
                  pl.BlockSpec((1, GLA_DV), lambda b, t: (0, 0)),
                  state_spec],
        out_specs=[pl.BlockSpec((Tt, hv), lambda b, t: (b * nT + t, 0)), state_spec],
        out_shape=[jax.ShapeDtypeStruct((M, hv), BF16),
                   jax.ShapeDtypeStruct((B, GLA_HEADS, GLA_DK, GLA_DV), F32)],
        scratch_shapes=[pltpu.VMEM((GLA_HEADS, GLA_DK, GLA_DV), F32)],
        compiler_params=_cp("parallel", "arbitrary"), name="gla")(
            proj, proj, proj, proj, la, norm_g, s0)


def _mm_ln_body(a_ref, w_ref, r_ref, g_ref, b_ref, of_ref, ob_ref):
    y = DEEPNORM_ALPHA * r_ref[...] + _dot(a_ref[...], w_ref[...])
    out = _layer_norm(y, g_ref[...], b_ref[...])
    of_ref[...] = out
    ob_ref[...] = out.astype(BF16)


def _mm_ln(a, w, resid, g, b, tm):
    M, K = a.shape
    N = w.shape[1]
    return pl.pallas_call(
        _mm_ln_body, grid=(M // tm,),
        in_specs=[pl.BlockSpec((tm, K), lambda i: (i, 0)),
                  pl.BlockSpec((K, N), lambda i: (0, 0)),
                  pl.BlockSpec((tm, N), lambda i: (i, 0)),
                  pl.BlockSpec((1, N), lambda i: (0, 0)),
                  pl.BlockSpec((1, N), lambda i: (0, 0))],
        out_specs=[pl.BlockSpec((tm, N), lambda i: (i, 0)),
                   pl.BlockSpec((tm, N), lambda i: (i, 0))],
        out_shape=[jax.ShapeDtypeStruct((M, N), F32), jax.ShapeDtypeStruct((M, N), BF16)],
        compiler_params=_cp("parallel"), name="mm_ln")(a, w, resid, g, b)


def _ffn_body(xb_ref, xf_ref, wg_ref, wu_ref, wd_ref, g_ref, b_ref, of_ref, ob_ref, acc_ref):
    j = pl.program_id(1)

    @pl.when(j == 0)
    def _():
        acc_ref[...] = jnp.zeros_like(acc_ref)

    x = xb_ref[...]
    hg = _dot(x, wg_ref[...])
    hu = _dot(x, wu_ref[...])
    h = (hg * jax.nn.sigmoid(hg)) * hu
    acc_ref[...] += _dot(h.astype(BF16), wd_ref[...])

    @pl.when(j == pl.num_programs(1) - 1)
    def _():
        out = _layer_norm(DEEPNORM_ALPHA * xf_ref[...] + acc_ref[...], g_ref[...], b_ref[...])
        of_ref[...] = out
        ob_ref[...] = out.astype(BF16)


def _ffn(xb, xf, wg, wu, wd, g, b, tm, tf):
    M = xb.shape[0]
    F = wg.shape[1]
    return pl.pallas_call(
        _ffn_body, grid=(M // tm, F // tf),
        in_specs=[pl.BlockSpec((tm, D_MODEL), lambda i, j: (i, 0)),
                  pl.BlockSpec((tm, D_MODEL), lambda i, j: (i, 0)),
                  pl.BlockSpec((D_MODEL, tf), lambda i, j: (0, j)),
                  pl.BlockSpec((D_MODEL, tf), lambda i, j: (0, j)),
                  pl.BlockSpec((tf, D_MODEL), lambda i, j: (j, 0)),
                  pl.BlockSpec((1, D_MODEL), lambda i, j: (0, 0)),
                  pl.BlockSpec((1, D_MODEL), lambda i, j: (0, 0))],
        out_specs=[pl.BlockSpec((tm, D_MODEL), lambda i, j: (i, 0)),
                   pl.BlockSpec((tm, D_MODEL), lambda i, j: (i, 0))],
        out_shape=[jax.ShapeDtypeStruct((M, D_MODEL), F32), jax.ShapeDtypeStruct((M, D_MODEL), BF16)],
        scratch_shapes=[pltpu.VMEM((tm, D_MODEL), F32)],
        compiler_params=_cp("parallel", "arbitrary"), name="ffn")(xb, xf, wg, wu, wd, g, b)


def _ple_body(xb_ref, xf_ref, pb_ref, wg_ref, wp_ref, of_ref, ob_ref):
    gate = jax.nn.sigmoid(_dot(xb_ref[...], wg_ref[...]))
    out = xf_ref[...] + gate * _dot(pb_ref[...], wp_ref[...])
    of_ref[...] = out
    ob_ref[...] = out.astype(BF16)


def _ple(xb, xf, pb, wg, wp, tm):
    M = xb.shape[0]
    return pl.pallas_call(
        _ple_body, grid=(M // tm,),
        in_specs=[pl.BlockSpec((tm, D_MODEL), lambda i: (i, 0)),
                  pl.BlockSpec((tm, D_MODEL), lambda i: (i, 0)),
                  pl.BlockSpec((tm, PLE_DIM), lambda i: (i, 0)),
                  pl.BlockSpec((D_MODEL, D_MODEL), lambda i: (0, 0)),
                  pl.BlockSpec((PLE_DIM, D_MODEL), lambda i: (0, 0))],
        out_specs=[pl.BlockSpec((tm, D_MODEL), lambda i: (i, 0)),
                   pl.BlockSpec((tm, D_MODEL), lambda i: (i, 0))],
        out_shape=[jax.ShapeDtypeStruct((M, D_MODEL), F32), jax.ShapeDtypeStruct((M, D_MODEL), BF16)],
        compiler_params=_cp("parallel"), name="ple")(xb, xf, pb, wg, wp)


def _router_body(x_ref, w_ref, info_ref, info_t_ref, cnt_ref):
    x = x_ref[...]
    xh, xl = _split2(x)
    w = w_ref[...]
    wh, wl = _split2(w)
    logits = _dot(xh, wh) + _dot(xl, wh) + _dot(xh, wl)
    lane = lax.broadcasted_iota(jnp.int32, logits.shape, 1)
    live = lane < N_EXPERTS
    lg = jnp.where(live, logits, -jnp.inf)
    m1 = jnp.max(lg, axis=-1, keepdims=True)
    i1 = jnp.min(jnp.where(lg == m1, lane, LANES), axis=-1, keepdims=True)
    lg2 = jnp.where(lane == i1, -jnp.inf, lg)
    m2 = jnp.max(lg2, axis=-1, keepdims=True)
    i2 = jnp.min(jnp.where(lg2 == m2, lane, LANES), axis=-1, keepdims=True)
    e2 = jnp.exp(m2 - m1)
    den = 1.0 + e2
    onehot = jnp.where((lane == i1) | (lane == i2), 1.0, 0.0)
    tb = x.shape[0]
    before = (lax.broadcasted_iota(jnp.int32, (tb, tb), 1) < lax.broadcasted_iota(jnp.int32, (tb, tb), 0))
    cum = _dot(before.astype(BF16), onehot.astype(BF16))
    r1 = jnp.sum(jnp.where(lane == i1, cum, 0.0), axis=-1, keepdims=True)
    r2 = jnp.sum(jnp.where(lane == i2, cum, 0.0), axis=-1, keepdims=True)
    cols = (i1.astype(F32), i2.astype(F32), 1.0 / den, e2 / den, r1, r2)
    info = jnp.zeros(logits.shape, F32)
    for c, val in enumerate(cols):
        info = jnp.where(lane == c, val, info)
    info_ref[...] = info
    info_t_ref[0] = info.T[0:8]
    cnt_ref[0] = jnp.broadcast_to(jnp.sum(onehot, axis=0, keepdims=True), (8, LANES))


def _router(xf, w_pad, tb):
    M = xf.shape[0]
    return pl.pallas_call(
        _router_body, grid=(M // tb,),
        in_specs=[pl.BlockSpec((tb, D_MODEL), lambda i: (i, 0)),
                  pl.BlockSpec((D_MODEL, LANES), lambda i: (0, 0))],
        out_specs=[pl.BlockSpec((tb, LANES), lambda i: (i, 0)),
                   pl.BlockSpec((1, 8, tb), lambda i: (i, 0, 0)),
                   pl.BlockSpec((1, 8, LANES), lambda i: (i, 0, 0))],
        out_shape=[jax.ShapeDtypeStruct((M, LANES), F32),
                   jax.ShapeDtypeStruct((M // tb, 8, tb), F32),
                   jax.ShapeDtypeStruct((M // tb, 8, LANES), F32)],
        compiler_params=_cp("parallel"), name="router")(xf, w_pad)


MOE_TILE = 64
MOE_GROUP = 8


def _moe_sizes(M, tb):
    nb = M // tb
    LT = 2 * tb // MOE_TILE + N_EXPERTS
    n_tiles = -(-(nb * LT) // MOE_GROUP) * MOE_GROUP + N_EXPERTS * MOE_GROUP
    return nb, LT, n_tiles, n_tiles // MOE_GROUP


def _moe_plan(counts, LT, n_tiles, n_groups):
    nt = (counts + MOE_TILE - 1) // MOE_TILE
    lend = jnp.cumsum(nt, axis=1)
    lstart = lend - nt
    lt = jnp.arange(LT, dtype=jnp.int32)
    used = lt[None, :] < lend[:, -1:]
    e_of = jnp.minimum(jnp.sum(lt[None, :, None] >= lend[:, None, :], axis=-1), N_EXPERTS - 1).astype(jnp.int32)
    k_in = lt[None, :] - jnp.take_along_axis(lstart, e_of, axis=1)
    tot = jnp.sum(nt, axis=0)
    tot_pad = (tot + MOE_GROUP - 1) // MOE_GROUP * MOE_GROUP
    gend = jnp.cumsum(tot_pad)
    goff = gend - tot_pad
    boff = jnp.cumsum(nt, axis=0) - nt
    dt = goff[e_of] + jnp.take_along_axis(boff, e_of, axis=1) + k_in
    g4 = jnp.arange(n_groups, dtype=jnp.int32) * MOE_GROUP
    nu = (gend[-1] // MOE_GROUP).astype(jnp.int32)
    g4 = jnp.minimum(g4, gend[-1] - MOE_GROUP)
    ge = jnp.sum(g4[:, None] >= gend[None, :], axis=-1).astype(jnp.int32)
    nb = counts.shape[0]
    ids = (jnp.arange(nb, dtype=jnp.int32)[:, None] * LT + lt[None, :])
    src = jnp.full((n_tiles + 1,), -1, jnp.int32).at[jnp.where(used, dt, n_tiles).reshape(-1)].set(ids.reshape(-1))
    src = src[:n_tiles].reshape(n_groups, MOE_GROUP)
    gt = jnp.where(src < 0, src[:, 0:1], src)
    return dict(le=jnp.where(used, e_of, -1).astype(jnp.int32),
                lr=(k_in * MOE_TILE).astype(jnp.int32),
                dt=jnp.where(used, dt, 0).astype(jnp.int32),
                gt=jnp.maximum(gt, 0).astype(jnp.int32), ge=ge, nu=nu.reshape(1))


def _moe_dispatch_body(le_ref, lr_ref, x_ref, info_t_ref, o_ref, *, LT):
    b = pl.program_id(0)
    it = info_t_ref[0]
    i1, i2, r1, r2 = it[0:1], it[1:2], it[4:5], it[5:6]
    sub = lax.broadcasted_iota(jnp.int32, (MOE_TILE, 1), 0)
    tiles = []
    for lt in range(LT):
        ef = le_ref[b, lt].astype(F32)
        slot = (lr_ref[b, lt] + sub).astype(F32)
        hit = ((i1 == ef) & (r1 == slot)) | ((i2 == ef) & (r2 == slot))
        tiles.append(jnp.where(hit, 1.0, 0.0).astype(BF16))
    o_ref[...] = _dot(jnp.concatenate(tiles, axis=0), x_ref[...]).astype(BF16)


def _moe_dispatch(xb, info_t, plan, tb):
    nb, LT = plan["le"].shape
    grid_spec = pltpu.PrefetchScalarGridSpec(
        num_scalar_prefetch=2, grid=(nb,),
        in_specs=[pl.BlockSpec((tb, D_MODEL), lambda b, le, lr: (b, 0)),
                  pl.BlockSpec((1, 8, tb), lambda b, le, lr: (b, 0, 0))],
        out_specs=pl.BlockSpec((LT * MOE_TILE, D_MODEL), lambda b, le, lr: (b, 0)))
    return pl.pallas_call(
        functools.partial(_moe_dispatch_body, LT=LT), grid_spec=grid_spec,
        out_shape=jax.ShapeDtypeStruct((nb * LT * MOE_TILE, D_MODEL), BF16),
        compiler_params=_cp("arbitrary"), name="moe_dispatch")(plan["le"], plan["lr"], xb, info_t)


def _moe_ffn_body(gt_ref, ge_ref, nu_ref, *refs):
    x_refs = refs[:MOE_GROUP]
    wg_ref, wu_ref, wd_ref, o_ref, x_ref, acc_ref = refs[MOE_GROUP:]
    j = pl.program_id(1)

    @pl.when(pl.program_id(0) < nu_ref[0])
    def _():
        @pl.when(j == 0)
        def _():
            acc_ref[...] = jnp.zeros_like(acc_ref)
            for k in range(MOE_GROUP):
                x_ref[k * MOE_TILE:(k + 1) * MOE_TILE, :] = x_refs[k][...]

        x = x_ref[...]
        hg = _dot(x, wg_ref[...])
        hu = _dot(x, wu_ref[...])
        h = (hg * jax.nn.sigmoid(hg)) * hu
        acc_ref[...] += _dot(h.astype(BF16), wd_ref[...])

        @pl.when(j == pl.num_programs(1) - 1)
        def _():
            o_ref[...] = acc_ref[...].astype(BF16)

    @pl.when((pl.program_id(0) >= nu_ref[0]) & (j == 0))
    def _():
        o_ref[...] = jnp.zeros_like(o_ref)


def _moe_ffn(xs, plan, wg, wu, wd, n_groups, tf):
    tm = MOE_TILE * MOE_GROUP
    F = wg.shape[2]
    nf = F // tf
    live = lambda g, nu: jnp.minimum(g, nu[0] - 1)
    col = lambda g, j, nu: jnp.where(g < nu[0], j, nf - 1)
    x_specs = [pl.BlockSpec((MOE_TILE, D_MODEL), functools.partial(
        lambda g, j, gt, ge, nu, k: (gt[live(g, nu), k], 0), k=k)) for k in range(MOE_GROUP)]
    grid_spec = pltpu.PrefetchScalarGridSpec(
        num_scalar_prefetch=3, grid=(n_groups, nf),
        in_specs=x_specs + [
            pl.BlockSpec((None, D_MODEL, tf), lambda g, j, gt, ge, nu: (ge[g], 0, col(g, j, nu))),
            pl.BlockSpec((None, D_MODEL, tf), lambda g, j, gt, ge, nu: (ge[g], 0, col(g, j, nu))),
            pl.BlockSpec((None, tf, D_MODEL), lambda g, j, gt, ge, nu: (ge[g], col(g, j, nu), 0))],
        out_specs=pl.BlockSpec((tm, D_MODEL), lambda g, j, gt, ge, nu: (g, 0)),
        scratch_shapes=[pltpu.VMEM((tm, D_MODEL), BF16), pltpu.VMEM((tm, D_MODEL), F32)])
    return pl.pallas_call(
        _moe_ffn_body, grid_spec=grid_spec,
        out_shape=jax.ShapeDtypeStruct((n_groups * tm, D_MODEL), BF16),
        compiler_params=_cp("arbitrary", "arbitrary"), name="moe_ffn")(
            plan["gt"], plan["ge"], plan["nu"], *([xs] * MOE_GROUP), wg, wu, wd)


def _moe_combine_body(le_ref, lr_ref, dt_ref, *refs, LT, KC):
    y_refs = refs[:LT]
    info_ref, xf_ref, g_ref, b_ref, of_ref, ob_ref = refs[LT:]
    b = pl.program_id(0)
    info = info_ref[...]
    i1, i2, w1, w2, r1, r2 = (info[:, c:c + 1] for c in range(6))
    per = LANES // MOE_TILE
    lane = lax.broadcasted_iota(jnp.int32, (1, LANES), 1)
    moe = None
    for c0 in range(0, LT, KC):
        pw, ys = [], []
        for s0 in range(c0, c0 + KC, per):
            ef = jnp.zeros((1, LANES), F32)
            slot = jnp.zeros((1, LANES), F32)
            for k in range(per):
                here = (lane >= k * MOE_TILE) & (lane < (k + 1) * MOE_TILE)
                ef = jnp.where(here, le_ref[b, s0 + k].astype(F32), ef)
                slot = jnp.where(here, (lr_ref[b, s0 + k] - k * MOE_TILE + lane).astype(F32), slot)
            pw.append((jnp.where((i1 == ef) & (r1 == slot), w1, 0.0)
                       + jnp.where((i2 == ef) & (r2 == slot), w2, 0.0)).astype(BF16))
            ys.extend(y_refs[s0 + k][...] for k in range(per))
        part = _dot(jnp.concatenate(pw, axis=1), jnp.concatenate(ys, axis=0))
        moe = part if moe is None else moe + part
    out = _layer_norm(DEEPNORM_ALPHA * xf_ref[...] + moe, g_ref[...], b_ref[...])
    of_ref[...] = out
    ob_ref[...] = out.astype(BF16)


def _moe_combine(ys, info, xf, plan, g, b, tb):
    M = xf.shape[0]
    nb, LT = plan["le"].shape
    KC = 8 if LT % 8 == 0 else LT
    blk = lambda bb, le, lr, dt: (bb, 0)
    const = lambda bb, le, lr, dt: (0, 0)
    y_specs = [pl.BlockSpec((MOE_TILE, D_MODEL), functools.partial(
        lambda bb, le, lr, dt, k: (dt[bb, k], 0), k=k)) for k in range(LT)]
    grid_spec = pltpu.PrefetchScalarGridSpec(
        num_scalar_prefetch=3, grid=(nb,),
        in_specs=y_specs + [pl.BlockSpec((tb, LANES), blk),
                            pl.BlockSpec((tb, D_MODEL), blk),
                            pl.BlockSpec((1, D_MODEL), const),
                            pl.BlockSpec((1, D_MODEL), const)],
        out_specs=[pl.BlockSpec((tb, D_MODEL), blk), pl.BlockSpec((tb, D_MODEL), blk)])
    return pl.pallas_call(
        functools.partial(_moe_combine_body, LT=LT, KC=KC), grid_spec=grid_spec,
        out_shape=[jax.ShapeDtypeStruct((M, D_MODEL), F32), jax.ShapeDtypeStruct((M, D_MODEL), BF16)],
        compiler_params=_cp("arbitrary"), name="moe_combine")(
            plan["le"], plan["lr"], plan["dt"], *([ys] * LT), info, xf, g, b)


def _nsa_kv_body(s_ref, w_ref, cos_ref, sin_ref, cmp_ref, *rest, tm, nT, with_kvx):
    rows = _dot(s_ref[...], w_ref[...])
    cos = cos_ref[...]
    sin = sin_ref[...]
    cmp_ref[...] = rows[:, 0:256]
    ks = _rope_lanes(rows[:, 256:384], cos, sin)
    vs = rows[:, 384:512]
    kw = _rope_lanes(rows[:, 512:640], cos, sin)
    vw = rows[:, 640:768]
    if not with_kvx:
        sel_ref, win_ref = rest
        sel_ref[:, 0:128] = ks
        sel_ref[:, 128:256] = vs
        win_ref[:, 0:128] = kw
        win_ref[:, 128:256] = vw
    else:
        kvx_ref, cmp_t_ref, sel_t_ref, win_t_ref = rest
        cmp_t_ref[0, 0] = rows[:, 0:128].T
        cmp_t_ref[0, 1] = rows[:, 128:256].T
        sel_t_ref[0, 0] = ks.T
        sel_t_ref[0, 1] = vs.T
        win_t_ref[0, 0] = kw.T
        win_t_ref[0, 1] = vw.T
        lane = lax.broadcasted_iota(jnp.int32, (tm, LANES), 1)
        pos = (pl.program_id(0) % nT) * tm + lax.broadcasted_iota(jnp.int32, (tm, LANES), 0)
        low = lane < NSA_HD
        onehot = jnp.where(lane - NSA_HD == _vdiv(pos, SEL_BLOCK), 1.0, 0.0)
        for g in range(NSA_GROUPS):
            pick = (lambda x: x) if g == 0 else (lambda x: pltpu.roll(x, NSA_HD, 1))
            kvx_ref[g, :, 0:128] = jnp.where(low, pick(ks), onehot).astype(BF16)
            kvx_ref[g, :, 128:256] = jnp.where(low, pick(vs), 0.0).astype(BF16)
            kvx_ref[g, :, 256:384] = jnp.where(low, pick(kw), 0.0).astype(BF16)
            kvx_ref[g, :, 384:512] = jnp.where(low, pick(vw), 0.0).astype(BF16)


def _nsa_kv(sb, w, cos, sin, tm, with_kvx):
    M = sb.shape[0]
    nT = cos.shape[0] // tm
    body = functools.partial(_nsa_kv_body, tm=tm, nT=nT, with_kvx=with_kvx)
    if with_kvx:
        T = cos.shape[0]
        t_spec = pl.BlockSpec((1, 2, LANES, tm), lambda i: (i // nT, 0, 0, i % nT))
        t_shape = jax.ShapeDtypeStruct((M // T, 2, LANES, T), F32)
        out_specs = [pl.BlockSpec((tm, 256), lambda i: (i, 0)),
                     pl.BlockSpec((NSA_GROUPS, tm, 512), lambda i: (0, i, 0)), t_spec, t_spec, t_spec]
        out_shape = [jax.ShapeDtypeStruct((M, 256), F32),
                     jax.ShapeDtypeStruct((NSA_GROUPS, M, 512), BF16), t_shape, t_shape, t_shape]
    else:
        out_specs = [pl.BlockSpec((tm, 256), lambda i: (i, 0))] * 3
        out_shape = [jax.ShapeDtypeStruct((M, 256), F32)] * 3
    return pl.pallas_call(
        body, grid=(M // tm,),
        in_specs=[pl.BlockSpec((tm, D_MODEL), lambda i: (i, 0)),
                  pl.BlockSpec((D_MODEL, 768), lambda i: (0, 0)),
                  pl.BlockSpec((tm, LANES), lambda i: (i % nT, 0)),
                  pl.BlockSpec((tm, LANES), lambda i: (i % nT, 0))],
        out_specs=out_specs, out_shape=out_shape,
        compiler_params=_cp("parallel"), name="nsa_kv")(sb, w, cos, sin)


def _nsa_q_body(x_ref, w_ref, cos_ref, sin_ref, o_ref):
    res = _dot(x_ref[...], w_ref[...])
    cos = cos_ref[...]
    sin = sin_ref[...]
    for h in range(NSA_HEADS):
        xh = res[:, h * LANES:(h + 1) * LANES]
        o_ref[h] = (_rope_lanes(xh, cos, sin) * (NSA_HD ** -0.5)).astype(o_ref.dtype)


def _nsa_q(xb, w_pad, cos, sin, tm, out_dtype):
    M = xb.shape[0]
    nT = cos.shape[0] // tm
    return pl.pallas_call(
        _nsa_q_body, grid=(M // tm,),
        in_specs=[pl.BlockSpec((tm, D_MODEL), lambda i: (i, 0)),
                  pl.BlockSpec((D_MODEL, NSA_HEADS * LANES), lambda i: (0, 0)),
                  pl.BlockSpec((tm, LANES), lambda i: (i % nT, 0)),
                  pl.BlockSpec((tm, LANES), lambda i: (i % nT, 0))],
        out_specs=pl.BlockSpec((NSA_HEADS, tm, LANES), lambda i: (0, i, 0)),
        out_shape=jax.ShapeDtypeStruct((NSA_HEADS, M, LANES), out_dtype),
        compiler_params=_cp("parallel"), name="nsa_q")(xb, w_pad, cos, sin)


def _nsa_gate_body(x_ref, w_ref, o_ref):
    o_ref[0] = jax.nn.sigmoid(_dot(x_ref[...], w_ref[0]))


def _nsa_gate(xb, w_pad, tm):
    M = xb.shape[0]
    S = w_pad.shape[0]
    return pl.pallas_call(
        _nsa_gate_body, grid=(M // tm, S),
        in_specs=[pl.BlockSpec((tm, D_MODEL), lambda i, s: (i, 0)),
                  pl.BlockSpec((1, D_MODEL, LANES), lambda i, s: (s, 0, 0))],
        out_specs=pl.BlockSpec((1, tm, LANES), lambda i, s: (s, i, 0)),
        out_shape=jax.ShapeDtypeStruct((S, M, LANES), F32),
        compiler_params=_cp("parallel", "arbitrary"), name="nsa_gate")(xb, w_pad)


def _compress_rows(load, n_rows, w1_refs, w2_refs, pe_refs):
    lane = lax.broadcasted_iota(jnp.int32, (n_rows, LANES), 1)
    low = lane < NSA_HD
    out = []
    for kv in range(2):
        w1 = w1_refs[kv][...]
        bias = (_dot(pe_refs[kv][0], w1)[0:1, 0:CMP_HIDDEN]
                + _dot(pe_refs[kv][1], w1)[0:1, CMP_HIDDEN:2 * CMP_HIDDEN])
        g0, g1 = [], []
        for rp in range(CMP_STRIDE // 2):
            p = load(kv, 2 * rp)
            q = load(kv, 2 * rp + 1)
            g0.append(jnp.where(low, p, pltpu.roll(q, NSA_HD, 1)).astype(BF16))
            g1.append(jnp.where(low, pltpu.roll(p, NSA_HD, 1), q).astype(BF16))
        per_g = []
        for cols in (g0, g1):
            ab = _dot(jnp.concatenate(cols, axis=1), w1)
            nxt = pltpu.roll(ab[:, CMP_HIDDEN:2 * CMP_HIDDEN], n_rows - 1, 0)
            h = ab[:, 0:CMP_HIDDEN] + nxt + bias
            per_g.append(_dot(jax.nn.gelu(h).astype(BF16), w2_refs[kv][...]))
        out.append(per_g)
    return out


def _compress_prompt_body(x_ref, w1k, w1v, w2k, w2v, pek, pev, cos_ref, sin_ref, o_ref, *, n_rows):
    load = lambda kv, r: x_ref[:, r * 256 + kv * LANES:r * 256 + (kv + 1) * LANES]
    c = _compress_rows(load, n_rows, (w1k, w1v), (w2k, w2v), (pek, pev))
    for g in range(NSA_GROUPS):
        o_ref[0, g, 0] = _rope_lanes(c[0][g], cos_ref[...], sin_ref[...]).astype(BF16)
        o_ref[0, g, 1] = c[1][g].astype(BF16)


def _compress_prompt(xflat, cw, cos, sin, B, n_rows):
    body = functools.partial(_compress_prompt_body, n_rows=n_rows)
    const2 = lambda b: (0, 0)
    const3 = lambda b: (0, 0, 0)
    return pl.pallas_call(
        body, grid=(B,),
        in_specs=[pl.BlockSpec((n_rows, 4096), lambda b: (b, 0)),
                  pl.BlockSpec((1024, 256), const2), pl.BlockSpec((1024, 256), const2),
                  pl.BlockSpec((CMP_HIDDEN, LANES), const2), pl.BlockSpec((CMP_HIDDEN, LANES), const2),
                  pl.BlockSpec((2, 16, 1024), const3), pl.BlockSpec((2, 16, 1024), const3),
                  pl.BlockSpec((n_rows, LANES), const2), pl.BlockSpec((n_rows, LANES), const2)],
        out_specs=pl.BlockSpec((1, NSA_GROUPS, 2, n_rows, LANES), lambda b: (b, 0, 0, 0, 0)),
        out_shape=jax.ShapeDtypeStruct((B, NSA_GROUPS, 2, n_rows, LANES), BF16),
        compiler_params=_cp("parallel"), name="compress_prompt")(
            xflat, cw["w1k"], cw["w1v"], cw["w2k"], cw["w2v"], cw["pek"], cw["pev"], cos, sin)


def _compress_sample_body(pt_ref, *refs, n_rows, PG):
    pages = refs[:PG]
    w1k, w1v, w2k, w2v, pek, pev, cos_ref, sin_ref, o_ref, x_ref = refs[PG:]
    j = pl.program_id(1)
    cpp = PAGE_SIZE // CMP_STRIDE
    orow = lax.broadcasted_iota(jnp.int32, (PAGE_SIZE, PAGE_SIZE), 0)
    pcol = lax.broadcasted_iota(jnp.int32, (PAGE_SIZE, PAGE_SIZE), 1)
    perm = (pcol == _vmod(orow, cpp) * CMP_STRIDE + _vdiv(orow, cpp)).astype(BF16)
    for k in range(PG):
        c0 = pl.multiple_of((j * PG + k) * cpp, cpp)
        for kv in range(2):
            xp = _dot_nt(perm, pages[k][0, kv].astype(BF16))
            for r in range(CMP_STRIDE):
                x_ref[kv, r, pl.ds(c0, cpp), :] = xp[r * cpp:(r + 1) * cpp, :]

    @pl.when(j == pl.num_programs(1) - 1)
    def _():
        load = lambda kv, r: x_ref[kv, r]
        c = _compress_rows(load, n_rows, (w1k, w1v), (w2k, w2v), (pek, pev))
        ck = c[0][0] + pltpu.roll(c[0][1], NSA_HD, 1)
        cv = c[1][0] + pltpu.roll(c[1][1], NSA_HD, 1)
        o_ref[0, :, 0:128] = _rope_lanes(ck, cos_ref[...], sin_ref[...]).astype(BF16)
        o_ref[0, :, 128:256] = cv.astype(BF16)


def _compress_sample(cache_t, page_table, cw, cos, sin, PG):
    B, n_pages = page_table.shape
    n_rows = n_pages * PAGE_SIZE // CMP_STRIDE
    body = functools.partial(_compress_sample_body, n_rows=n_rows, PG=PG)
    const2 = lambda b, j, pt: (0, 0)
    const3 = lambda b, j, pt: (0, 0, 0)
    page_specs = [pl.BlockSpec((1, 2, LANES, PAGE_SIZE), functools.partial(
        lambda b, j, pt, k: (pt[b, j * PG + k], 0, 0, 0), k=k)) for k in range(PG)]
    grid_spec = pltpu.PrefetchScalarGridSpec(
        num_scalar_prefetch=1, grid=(B, n_pages // PG),
        in_specs=page_specs + [
            pl.BlockSpec((1024, 256), const2), pl.BlockSpec((1024, 256), const2),
            pl.BlockSpec((CMP_HIDDEN, LANES), const2), pl.BlockSpec((CMP_HIDDEN, LANES), const2),
            pl.BlockSpec((2, 16, 1024), const3), pl.BlockSpec((2, 16, 1024), const3),
            pl.BlockSpec((n_rows, LANES), const2), pl.BlockSpec((n_rows, LANES), const2)],
        out_specs=pl.BlockSpec((1, n_rows, 256), lambda b, j, pt: (b, 0, 0)),
        scratch_shapes=[pltpu.VMEM((2, CMP_STRIDE, n_rows, LANES), F32)])
    return pl.pallas_call(
        body, grid_spec=grid_spec,
        out_shape=jax.ShapeDtypeStruct((B, n_rows, 256), BF16),
        compiler_params=_cp("parallel", "arbitrary"), name="compress_sample")(
            page_table, *([cache_t] * PG), cw["w1k"], cw["w1v"], cw["w2k_lo"], cw["w2v_lo"],
            cw["pek"], cw["pev"], cos, sin)


def _nsa_attn_body(q_ref, gate_ref, ckv_ref, ovt_ref, kvx_ref, o_ref,
                   s_ref, p_ref, m_ref, l_ref, a_ref, acc_ref, *, Tq, Tk, n_cmp, n_sel):
    q0 = pl.program_id(2) * Tq
    R = NSA_HPG * Tq
    NB = 32
    q3 = q_ref[...]
    qh = q3.reshape(R, LANES)
    tq_lane = q0 + lax.broadcasted_iota(jnp.int32, (1, Tq), 1)
    t_lane = jnp.concatenate([tq_lane] * NSA_HPG, axis=1)

    ck = ckv_ref[0, 0, 0]
    cv = ckv_ref[0, 0, 1]
    n_rows = ck.shape[0]
    s = _dot_nt(ck, qh)
    nidx = lax.broadcasted_iota(jnp.int32, (n_rows, 1), 0)
    cmask = (nidx < n_cmp) & (CMP_STRIDE * nidx + (CMP_LEN - 1) <= t_lane)
    s = jnp.where(cmask, s, NEG_INF)
    e = jnp.where(cmask, jnp.exp(s - jnp.max(s, axis=0, keepdims=True)), 0.0)
    p = e / jnp.maximum(jnp.sum(e, axis=0, keepdims=True), 1e-30)
    o_cmp = _dot_tn(cv, p.astype(BF16))
    psum = p[:, 0:Tq]
    for h in range(1, NSA_HPG):
        psum = psum + p[:, h * Tq:(h + 1) * Tq]
    ph, plo = _split2(psum)
    imp = _dot(ovt_ref[...], ph) + _dot(ovt_ref[...], plo)

    sc = imp[NSA_HD:NSA_HD + NB, :]
    jj = lax.broadcasted_iota(jnp.int32, (NB, 1), 0)
    cur = _vdiv(tq_lane, SEL_BLOCK)
    forced = (jj == 0) | (jj == cur) | (jj == cur - 1)
    valid = SEL_BLOCK * jj <= tq_lane
    sc = jnp.where(valid, jnp.where(forced, FORCE_SCORE, sc), -FORCE_SCORE)
    rank = jnp.zeros((NB, Tq), F32)
    for jp in range(n_sel):
        c = sc[jp:jp + 1, :]
        beats = (c > sc) | ((c == sc) & (jp < jj))
        rank = rank + jnp.where(beats, 1.0, 0.0)
    drop = (rank >= float(min(SEL_TOP, n_sel))) & (jj < n_sel)
    bias_t = jnp.concatenate([jnp.zeros((NSA_HD, Tq), F32), jnp.where(drop, NEG_INF, 0.0),
                              jnp.zeros((LANES - NSA_HD - NB, Tq), F32)], axis=0)
    qa = (q3.astype(F32) + bias_t.T[None]).astype(BF16).reshape(R, LANES)

    def reset():
        m_ref[...] = jnp.full(m_ref.shape, NEG_INF, F32)
        l_ref[...] = jnp.zeros(l_ref.shape, F32)
        acc_ref[...] = jnp.zeros(acc_ref.shape, F32)

    def flash_tile(kt, qv, kcol, masked, window):
        k0 = pl.multiple_of(kt * Tk, Tk)
        ka = kvx_ref[0, pl.ds(k0, Tk), kcol:kcol + LANES]
        va = kvx_ref[0, pl.ds(k0, Tk), kcol + LANES:kcol + 2 * LANES]
        s_ref[...] = _dot_nt(ka, qv)
        if masked:
            kpos = k0 + lax.broadcasted_iota(jnp.int32, (Tk, 1), 0)
            ok = kpos <= tq_lane
            if window:
                ok = ok & (kpos > tq_lane - WINDOW)
        for h in range(NSA_HPG):
            cols = slice(h * Tq, (h + 1) * Tq)
            s = s_ref[:, cols]
            if masked:
                s = jnp.where(ok, s, NEG_INF)
            m_old = m_ref[0:1, cols]
            m_new = jnp.maximum(m_old, jnp.max(s, axis=0, keepdims=True))
            a = jnp.exp(m_old - m_new)
            p = jnp.exp(s - m_new)
            l_ref[0:1, cols] = a * l_ref[0:1, cols] + jnp.sum(p, axis=0, keepdims=True)
            m_ref[0:1, cols] = m_new
            a_ref[0:1, cols] = a
            p_ref[:, cols] = p.astype(BF16)
        acc_ref[...] = a_ref[0:1, :] * acc_ref[...] + _dot_tn(va, p_ref[...])

    def run(lo, hi, qv, kcol, masked, window):
        def step(kt, carry):
            flash_tile(kt, qv, kcol, masked, window)
            return carry
        lax.fori_loop(lo, hi, step, 0)

    kd = q0 // Tk
    reset()
    run(0, kd, qa, 0, False, False)
    flash_tile(kd, qa, 0, True, False)
    o_sel = acc_ref[...] / l_ref[0:1, :]

    reset()
    run(jnp.maximum(q0 - WINDOW + 1, 0) // Tk, kd + 1, qh, 2 * LANES, True, True)
    o_win = acc_ref[...] / l_ref[0:1, :]

    gates_t = gate_ref[0].T
    heads = []
    for h in range(NSA_HPG):
        cols = slice(h * Tq, (h + 1) * Tq)
        o_t = (gates_t[h:h + 1, :] * o_cmp[:, cols]
               + gates_t[NSA_HPG + h:NSA_HPG + h + 1, :] * o_sel[:, cols]
               + gates_t[2 * NSA_HPG + h:2 * NSA_HPG + h + 1, :] * o_win[:, cols])
        heads.append(o_t.T)
    low = lax.broadcasted_iota(jnp.int32, (1, LANES), 1) < NSA_HD
    for hp in range(NSA_HPG // 2):
        pair = jnp.where(low, heads[2 * hp], pltpu.roll(heads[2 * hp + 1], NSA_HD, 1))
        o_ref[:, hp * LANES:(hp + 1) * LANES] = pair.astype(o_ref.dtype)


def _nsa_attn_prompt(q, gates, ckv, ov, kvx, B, T, Tq, Tk, n_cmp, n_sel):
    nq = T // Tq
    M = B * T
    n_rows = ckv.shape[3]
    assert Tk % Tq == 0 and Tq % LANES == 0 and n_sel <= 32
    R = NSA_HPG * Tq
    body = functools.partial(_nsa_attn_body, Tq=Tq, Tk=Tk, n_cmp=n_cmp, n_sel=n_sel)
    scratch = [pltpu.VMEM((Tk, R), F32), pltpu.VMEM((Tk, R), BF16), pltpu.VMEM((8, R), F32),
               pltpu.VMEM((8, R), F32), pltpu.VMEM((8, R), F32), pltpu.VMEM((LANES, R), F32)]
    return pl.pallas_call(
        body, grid=(B, NSA_GROUPS, nq),
        in_specs=[pl.BlockSpec((NSA_HPG, Tq, LANES), lambda b, g, i: (g, b * nq + i, 0)),
                  pl.BlockSpec((1, Tq, LANES), lambda b, g, i: (g, b * nq + i, 0)),
                  pl.BlockSpec((1, 1, 2, n_rows, LANES), lambda b, g, i: (b, g, 0, 0, 0)),
                  pl.BlockSpec((LANES, n_rows), lambda b, g, i: (0, 0)),
                  pl.BlockSpec((1, T, 512), lambda b, g, i: (g, b, 0))],
        out_specs=pl.BlockSpec((Tq, NSA_HPG * NSA_HD), lambda b, g, i: (b * nq + i, g)),
        out_shape=jax.ShapeDtypeStruct((M, NSA_HEADS * NSA_HD), BF16),
        scratch_shapes=scratch,
        compiler_params=_cp("parallel", "parallel", "arbitrary"), name="nsa_attn_prompt")(
            q, gates, ckv, ov, kvx)


def _col_softmax_update(s, m, l, acc, v):
    m_new = jnp.maximum(m, jnp.max(s, axis=0, keepdims=True))
    a = jnp.exp(m - m_new)
    p = jnp.exp(s - m_new)
    l = a * l + jnp.sum(p, axis=0, keepdims=True)
    return m_new, l, a, _dot_tn(p.astype(BF16), v)


def _nsa_attn_sample_body(pt_ref, *refs, T, PG, past_len, n_cmp, n_sel, n_rows, wb):
    pages = refs[:PG]
    (q_ref, gate_ref, ckv_ref, ovt_ref, swin_ref, wnew_ref, snew_ref, o_ref,
     qb_ref, selb_ref, m_ref, l_ref, acc_ref, oc_ref, ow_ref) = refs[PG:]
    j = pl.program_id(1)
    NQ = NSA_HEADS * T
    lane = lax.broadcasted_iota(jnp.int32, (1, NQ), 1)
    tl = past_len + _vmod(lane, T)
    eye = (lax.broadcasted_iota(jnp.int32, (NQ, NQ), 0) == lax.broadcasted_iota(jnp.int32, (NQ, NQ), 1))

    def to_rows(x):
        return jnp.sum(jnp.where(eye, x, 0.0), axis=1, keepdims=True)

    @pl.when(j == 0)
    def _():
        qp = q_ref[...].reshape(NQ, LANES)
        rowi = lax.broadcasted_iota(jnp.int32, (NQ, LANES), 0)
        qb = jnp.where(rowi < NQ // 2, qp, pltpu.roll(qp, NSA_HD, 1)).astype(BF16)
        qb_ref[...] = qb

        ck = ckv_ref[0, :, 0:128]
        cv = ckv_ref[0, :, 128:256]
        s = _dot_nt(ck, qb)
        nidx = lax.broadcasted_iota(jnp.int32, (n_rows, 1), 0)
        cmask = (nidx < n_cmp) & (CMP_STRIDE * nidx + (CMP_LEN - 1) <= tl)
        s = jnp.where(cmask, s, NEG_INF)
        e = jnp.where(cmask, jnp.exp(s - jnp.max(s, axis=0, keepdims=True)), 0.0)
        p = e / jnp.maximum(jnp.sum(e, axis=0, keepdims=True), 1e-30)
        oc_ref[...] = _dot_tn(p.astype(BF16), cv)
        ri = lax.broadcasted_iota(jnp.int32, (NQ, NQ), 0)
        ci = lax.broadcasted_iota(jnp.int32, (NQ, NQ), 1)
        same = ((_vdiv(ri, NQ // 2) == _vdiv(ci, NQ // 2)) & (_vmod(ri, T) == _vmod(ci, T))).astype(BF16)
        ph, plo = _split2(p)
        psum = _dot(ph, same) + _dot(plo, same)
        sh, slo = _split2(psum)
        imp = _dot(ovt_ref[...], sh) + _dot(ovt_ref[...], slo)

        nb = imp.shape[0]
        jj = lax.broadcasted_iota(jnp.int32, (nb, 1), 0)
        cur = _vdiv(tl, SEL_BLOCK)
        forced = (jj == 0) | (jj == cur) | (jj == cur - 1)
        valid = SEL_BLOCK * jj <= tl
        score = jnp.where(valid, jnp.where(forced, FORCE_SCORE, imp), -FORCE_SCORE)
        work = jnp.where(jj < n_sel, score, -2e38)
        keep = jnp.zeros(work.shape, jnp.bool_)
        for _ in range(min(SEL_TOP, n_sel)):
            mx = jnp.max(work, axis=0, keepdims=True)
            first = jnp.min(jnp.where(work == mx, jj, nb), axis=0, keepdims=True)
            pick = jj == first
            keep = keep | pick
            work = jnp.where(pick, -3e38, work)
        selb_ref[...] = jnp.where(keep, 0.0, NEG_INF)

        kw = swin_ref[0, 0].T.astype(BF16)
        vw = swin_ref[0, 1].T.astype(BF16)
        sw = _dot_nt(kw, qb)
        kpos = past_len - wb + lax.broadcasted_iota(jnp.int32, (wb, 1), 0)
        sw = jnp.where((kpos <= tl) & (kpos > tl - WINDOW), sw, NEG_INF)
        zero = jnp.zeros((T, 256), F32)
        wn = jnp.concatenate([wnew_ref[...], zero], axis=0)
        sn = _dot_nt(wn[:, 0:128].astype(BF16), qb)
        kposn = past_len + lax.broadcasted_iota(jnp.int32, (2 * T, 1), 0)
        sn = jnp.where((kposn <= tl) & (kposn > tl - WINDOW), sn, NEG_INF)
        mw = jnp.maximum(jnp.max(sw, axis=0, keepdims=True), jnp.max(sn, axis=0, keepdims=True))
        pw = jnp.exp(sw - mw)
        pn = jnp.exp(sn - mw)
        lw = jnp.sum(pw, axis=0, keepdims=True) + jnp.sum(pn, axis=0, keepdims=True)
        ow = _dot_tn(pw.astype(BF16), vw) + _dot_tn(pn.astype(BF16), wn[:, 128:256].astype(BF16))
        ow_ref[...] = ow / to_rows(lw)

        m_ref[...] = jnp.full(m_ref.shape, NEG_INF, F32)
        l_ref[...] = jnp.zeros(l_ref.shape, F32)
        acc_ref[...] = jnp.zeros(acc_ref.shape, F32)

    qb = qb_ref[...]
    kt = jnp.concatenate([pg[0, 0].T.astype(BF16) for pg in pages], axis=0)
    vt = jnp.concatenate([pg[0, 1].T.astype(BF16) for pg in pages], axis=0)
    blk_per_step = PG * PAGE_SIZE // SEL_BLOCK
    sb = selb_ref[pl.ds(pl.multiple_of(j * blk_per_step, blk_per_step), blk_per_step), :]
    sb = jnp.broadcast_to(sb[:, None, :], (blk_per_step, SEL_BLOCK, NQ)).reshape(PG * PAGE_SIZE, NQ)
    sc = _dot_nt(kt, qb) + sb
    m_new, l_new, a, pv = _col_softmax_update(sc, m_ref[0:1, :], l_ref[0:1, :], None, vt)
    m_ref[0:1, :] = m_new
    l_ref[0:1, :] = l_new
    acc_ref[...] = to_rows(a) * acc_ref[...] + pv

    @pl.when(j == pl.num_programs(1) - 1)
    def _():
        zero = jnp.zeros((T, 256), F32)
        sn_rows = jnp.concatenate([snew_ref[...], zero], axis=0)
        kposn = past_len + lax.broadcasted_iota(jnp.int32, (2 * T, 1), 0)
        n_past_blk = past_len // SEL_BLOCK
        sbn = selb_ref[n_past_blk:n_past_blk + 1, :]
        sc = _dot_nt(sn_rows[:, 0:128].astype(BF16), qb) + sbn
        sc = jnp.where(kposn <= tl, sc, NEG_INF)
        m_new, l_new, a, pv = _col_softmax_update(sc, m_ref[0:1, :], l_ref[0:1, :], None,
                                                  sn_rows[:, 128:256].astype(BF16))
        o_sel = (to_rows(a) * acc_ref[...] + pv) / to_rows(l_new)
        gates = gate_ref[...].reshape(NQ, LANES)
        o = gates[:, 0:1] * oc_ref[...] + gates[:, 1:2] * o_sel + gates[:, 2:3] * ow_ref[...]
        low = lax.broadcasted_iota(jnp.int32, (T, LANES), 1) < NSA_HD
        for hp in range(NSA_HEADS // 2):
            a0 = o[(2 * hp) * T:(2 * hp + 1) * T]
            a1 = o[(2 * hp + 1) * T:(2 * hp + 2) * T]
            if 2 * hp < NSA_HPG:
                piece = jnp.where(low, a0, pltpu.roll(a1, NSA_HD, 1))
            else:
                piece = jnp.where(low, pltpu.roll(a0, NSA_HD, 1), a1)
            o_ref[:, hp * LANES:(hp + 1) * LANES] = piece


def _nsa_attn_sample(q, gates, ckv, ovt, cache_sel_t, page_table, state_win_t, win_new, sel_new, T, PG):
    B, n_pages = page_table.shape
    past_len = n_pages * PAGE_SIZE
    n_rows = ckv.shape[1]
    wb = state_win_t.shape[3]
    n_cmp = (past_len + T) // CMP_STRIDE - 1
    n_sel = past_len // SEL_BLOCK + -(-T // SEL_BLOCK)
    nb = ovt.shape[0]
    NQ = NSA_HEADS * T
    body = functools.partial(_nsa_attn_sample_body, T=T, PG=PG, past_len=past_len, n_cmp=n_cmp,
                             n_sel=n_sel, n_rows=n_rows, wb=wb)
    page_specs = [pl.BlockSpec((1, 2, LANES, PAGE_SIZE), functools.partial(
        lambda b, j, pt, k: (pt[b, j * PG + k], 0, 0, 0), k=k)) for k in range(PG)]
    grid_spec = pltpu.PrefetchScalarGridSpec(
        num_scalar_prefetch=1, grid=(B, n_pages // PG),
        in_specs=page_specs + [
            pl.BlockSpec((NSA_HEADS, T, LANES), lambda b, j, pt: (0, b, 0)),
            pl.BlockSpec((NSA_HEADS, T, LANES), lambda b, j, pt: (0, b, 0)),
            pl.BlockSpec((1, n_rows, 256), lambda b, j, pt: (b, 0, 0)),
            pl.BlockSpec((nb, n_rows), lambda b, j, pt: (0, 0)),
            pl.BlockSpec((1, 2, LANES, wb), lambda b, j, pt: (b, 0, 0, 0)),
            pl.BlockSpec((T, 256), lambda b, j, pt: (b, 0)),
            pl.BlockSpec((T, 256), lambda b, j, pt: (b, 0))],
        out_specs=pl.BlockSpec((T, NSA_HEADS * NSA_HD), lambda b, j, pt: (b, 0)),
        scratch_shapes=[pltpu.VMEM((NQ, LANES), BF16), pltpu.VMEM((nb, NQ), F32),
                        pltpu.VMEM((8, NQ), F32), pltpu.VMEM((8, NQ), F32),
                        pltpu.VMEM((NQ, LANES), F32), pltpu.VMEM((NQ, LANES), F32),
                        pltpu.VMEM((NQ, LANES), F32)])
    return pl.pallas_call(
        body, grid_spec=grid_spec,
        out_shape=jax.ShapeDtypeStruct((B * T, NSA_HEADS * NSA_HD), F32),
        compiler_params=_cp("parallel", "arbitrary"), name="nsa_attn_sample")(
            page_table, *([cache_sel_t] * PG), q, gates, ckv, ovt, state_win_t, win_new, sel_new)


def _rope_tables(pos, pattern):
    half = NSA_HD // 2
    inv = ROPE_THETA ** (-jnp.arange(half, dtype=F32) / half)
    ang = pos.astype(F32)[:, None] * inv[None, :]
    cos, sin = jnp.cos(ang), jnp.sin(ang)
    zero = jnp.zeros_like(cos)
    cs = [jnp.concatenate([cos, cos], 1) if p else jnp.concatenate([zero, zero], 1) for p in pattern]
    sn = [jnp.concatenate([-sin, sin], 1) if p else jnp.concatenate([zero, zero], 1) for p in pattern]
    return jnp.concatenate(cs, 1), jnp.concatenate(sn, 1)


def _overlap(n_rows, n_sel):
    ci = jnp.arange(n_rows, dtype=jnp.int32)[:, None] * CMP_STRIDE
    sj = jnp.arange(n_sel, dtype=jnp.int32)[None, :] * SEL_BLOCK
    return ((ci < sj + SEL_BLOCK) & (ci + CMP_LEN > sj)).astype(F32)


def _prep_weights(w):
    d = D_MODEL
    hk, hv = GLA_HEADS * GLA_DK, GLA_HEADS * GLA_DV
    p = {}
    p["ln_g"] = w["ln_g"].reshape(DEPTH, 2, 1, d)
    p["ln_b"] = w["ln_b"].reshape(DEPTH, 2, 1, d)
    w_in = w["gla_w_in"][0]
    p["gla_w_qkvr"] = w_in[:, :2 * hk + 2 * hv].astype(BF16)
    p["gla_w_a"] = jnp.pad(w_in[:, 2 * hk + 2 * hv:], ((0, 0), (0, LANES - GLA_GATE_RANK))).astype(BF16)
    p["gla_w_a2"] = jnp.pad(w["gla_w_a2"][0], ((0, LANES - GLA_GATE_RANK), (0, 0))).astype(BF16)
    p["gla_b_a"] = w["gla_b_a"][0].reshape(1, hk)
    p["gla_norm_g"] = w["gla_norm_g"][0].reshape(1, GLA_DV)
    p["gla_w_o"] = w["gla_w_o"][0].astype(BF16)
    p["nsa_w_kv"] = w["nsa_w_kv"].astype(BF16)
    nw = w["nsa_w_in"][0]
    wq = nw[:, :NSA_HEADS * NSA_HD].reshape(d, NSA_HEADS, NSA_HD)
    p["nsa_w_q"] = jnp.pad(wq, ((0, 0), (0, 0), (0, LANES - NSA_HD))).reshape(d, NSA_HEADS * LANES).astype(BF16)
    wg = nw[:, NSA_HEADS * NSA_HD:].reshape(d, 3, NSA_GROUPS, NSA_HPG)
    wg_grp = jnp.transpose(wg, (2, 0, 1, 3)).reshape(NSA_GROUPS, d, 3 * NSA_HPG)
    p["nsa_w_gate_grp"] = jnp.pad(wg_grp, ((0, 0), (0, 0), (0, LANES - 3 * NSA_HPG))).astype(BF16)
    wg_head = jnp.transpose(wg, (2, 3, 0, 1)).reshape(NSA_HEADS, d, 3)
    p["nsa_w_gate_head"] = jnp.pad(wg_head, ((0, 0), (0, 0), (0, LANES - 3))).astype(BF16)
    p["nsa_w_o"] = w["nsa_w_o"][0].astype(BF16)
    cw = {}
    for name, pe, w1, w2 in (("k", w["cmp_pe_k"], w["cmp_w1_k"], w["cmp_w2_k"]),
                             ("v", w["cmp_pe_v"], w["cmp_w1_v"], w["cmp_w2_v"])):
        half = CMP_STRIDE * NSA_HD
        cw["w1" + name] = jnp.concatenate([w1[:half], w1[half:]], axis=1).astype(BF16)
        cw["w2" + name] = jnp.pad(w2, ((0, 0), (0, LANES - NSA_HD))).astype(BF16)
        cw["w2" + name + "_lo"] = cw["w2" + name]
        pe2 = pe.reshape(2, 1, half)
        cw["pe" + name] = jnp.broadcast_to(pe2, (2, 16, half)).astype(BF16)
    p["cmp"] = cw
    p["ffn_w_gate"] = w["ffn_w_gate"][0].astype(BF16)
    p["ffn_w_up"] = w["ffn_w_up"][0].astype(BF16)
    p["ffn_w_down"] = w["ffn_w_down"][0].astype(BF16)
    p["moe_w_router"] = jnp.pad(w["moe_w_router"][0], ((0, 0), (0, LANES - N_EXPERTS)))
    p["moe_w_gate"] = w["moe_w_gate"][0].astype(BF16)
    p["moe_w_up"] = w["moe_w_up"][0].astype(BF16)
    p["moe_w_down"] = w["moe_w_down"][0].astype(BF16)
    p["ple_w_proj"] = w["ple_w_proj"].astype(BF16)
    p["ple_w_gate"] = w["ple_w_gate"].astype(BF16)
    return p


def _tile(M, cap):
    t = cap
    while M % t:
        t //= 2
    return t


def _trunk(x, ple, pos0, gla_s0, past, p):
    B, T, d = x.shape
    M = B * T
    tm = _tile(M, 1024)
    tm_s = _tile(M, 512)
    xf = x.reshape(M, d)
    xb = xf.astype(BF16)
    pleb = ple.reshape(DEPTH, M, PLE_DIM).astype(BF16)

    proj = _proj(xb, p["gla_w_qkvr"], tm, 512, F32)
    la = _gla_gate(xb, p["gla_w_a"], p["gla_w_a2"], p["gla_b_a"], tm)
    if T >= GLA_CHUNK:
        C, NC, Tp = GLA_CHUNK, _tile(T // GLA_CHUNK, 4), T
    else:
        C, NC, Tp = 16, 1, 16
        pad = lambda a: jnp.pad(a.reshape(B, T, -1), ((0, 0), (0, Tp - T), (0, 0))).reshape(B * Tp, -1)
        proj, la = pad(proj), pad(la)
    o, s_new = _gla(proj, la, p["gla_norm_g"], gla_s0, B, Tp, C, NC)
    if Tp != T:
        o = o.reshape(B, Tp, -1)[:, :T].reshape(M, -1)
    x1f, x1b = _mm_ln(o, p["gla_w_o"], xf, p["ln_g"][0, 0], p["ln_b"][0, 0], tm_s)
    x2f, x2b = _ffn(x1b, x1f, p["ffn_w_gate"], p["ffn_w_up"], p["ffn_w_down"],
                    p["ln_g"][0, 1], p["ln_b"][0, 1], tm_s, 1408)
    x3f, x3b = _ple(x2b, x2f, pleb[0], p["ple_w_gate"][0], p["ple_w_proj"][0], tm_s)

    qpos = pos0 + jnp.arange(T, dtype=jnp.int32)
    cos_kv, sin_kv = _rope_tables(qpos, (1, 1))
    cos_q, sin_q = _rope_tables(qpos, (1, 0))
    if past is None:
        tq = _tile(T, 512)
    else:
        tq = _tile(M, 512)
        cos_kv, sin_kv, cos_q, sin_q = (jnp.tile(a, (B, 1)) for a in (cos_kv, sin_kv, cos_q, sin_q))
    kv = _nsa_kv(x3b, p["nsa_w_kv"], cos_kv, sin_kv, tq, past is None)
    cmp_rows = kv[0]
    from_t = lambda a: jnp.transpose(a.reshape(B, 2, NSA_GROUPS, NSA_HD, a.shape[-1]), (0, 4, 1, 2, 3))
    to_t = lambda a: jnp.transpose(a, (0, 2, 3, 4, 1)).reshape(a.shape[0], 2, NSA_GROUPS * NSA_HD, a.shape[1])
    shp = (B, T, 2, NSA_GROUPS, NSA_HD)
    if past is None:
        n_ch = T // CMP_STRIDE
        n_cmp = n_ch - 1
        n_sel = T // SEL_BLOCK
        cpos = jnp.arange(n_ch, dtype=jnp.int32) * CMP_STRIDE
        cos_c, sin_c = _rope_tables(cpos, (1, 0))
        ckv = _compress_prompt(cmp_rows.reshape(B * n_ch, CMP_STRIDE * 256), p["cmp"], cos_c, sin_c, B, n_ch)
        ov = jnp.pad(_overlap(n_ch, n_sel).T, ((NSA_HD, LANES - NSA_HD - n_sel), (0, 0))).astype(BF16)
        q = _nsa_q(x3b, p["nsa_w_q"], cos_q, sin_q, tq, BF16)
        gates = _nsa_gate(x3b, p["nsa_w_gate_grp"], tm_s)
        o = _nsa_attn_prompt(q, gates, ckv, ov, kv[1], B, T, _tile(T, 128), _tile(T, 512), n_cmp, n_sel)
        cmp_out, sel_out = from_t(kv[2]), from_t(kv[3])
        win_out = from_t(kv[4][..., T - min(WINDOW, T):])
    else:
        cache_cmp, cache_sel, page_table, state_win = past
        n_pool = cache_cmp.shape[0]
        n_pages = page_table.shape[1]
        past_len = n_pages * PAGE_SIZE
        PG = _tile(n_pages, 16)
        n_rows = past_len // CMP_STRIDE
        n_sel = past_len // SEL_BLOCK + -(-T // SEL_BLOCK)
        cpos = jnp.arange(n_rows, dtype=jnp.int32) * CMP_STRIDE
        cos_c, sin_c = _rope_tables(cpos, (1, 1))
        sel_rows, win_rows = kv[1], kv[2]
        ckv = _compress_sample(to_t(cache_cmp), page_table, p["cmp"], cos_c, sin_c, PG)
        nb = -(-n_sel // LANES) * LANES
        ovt = jnp.pad(_overlap(n_rows, n_sel).T, ((0, nb - n_sel), (0, 0))).astype(BF16)
        assert T < CMP_STRIDE and T <= SEL_BLOCK
        q = _nsa_q(x3b, p["nsa_w_q"], cos_q, sin_q, tq, F32)
        gates = _nsa_gate(x3b, p["nsa_w_gate_head"], tm_s)
        state_win_t = to_t(state_win)
        o = _nsa_attn_sample(q, gates, ckv, ovt, to_t(cache_sel), page_table,
                             state_win_t, win_rows, sel_rows, T, PG).astype(BF16)
        cmp_out, sel_out = cmp_rows.reshape(shp), sel_rows.reshape(shp)
        win_out = from_t(jnp.concatenate([state_win_t, to_t(win_rows.reshape(shp))], axis=-1)[..., T:])
    x4f, x4b = _mm_ln(o, p["nsa_w_o"], x3f, p["ln_g"][1, 0], p["ln_b"][1, 0], tm_s)
    tb = _tile(M, 1024)
    nb, LT, n_tiles, n_groups = _moe_sizes(M, tb)
    info, info_t, cnt = _router(x4f, p["moe_w_router"], tb)
    plan = _moe_plan(cnt[:, 0, :N_EXPERTS].astype(jnp.int32), LT, n_tiles, n_groups)
    xs = _moe_dispatch(x4b, info_t, plan, tb)
    ys = _moe_ffn(xs, plan, p["moe_w_gate"], p["moe_w_up"], p["moe_w_down"], n_groups, 1792)
    x5f, x5b = _moe_combine(ys, info, x4f, plan, p["ln_g"][1, 1], p["ln_b"][1, 1], tb)
    x6f, _ = _ple(x5b, x5f, pleb[1], p["ple_w_gate"][1], p["ple_w_proj"][1], tm_s)

    return (x6f.reshape(B, T, d), s_new[None], cmp_out, sel_out, win_out)


def kernel(x_prompt, x_sample, state_gla, cache_cmp, cache_sel, state_win, page_table, p_prompt, p_sample,
           ln_g, ln_b, gla_w_in, gla_w_a2, gla_b_a, gla_norm_g, gla_w_o,
           nsa_w_kv, cmp_pe_k, cmp_w1_k, cmp_w2_k, cmp_pe_v, cmp_w1_v, cmp_w2_v,
           nsa_w_in, nsa_w_o, ffn_w_gate, ffn_w_up, ffn_w_down,
           moe_w_router, moe_w_gate, moe_w_up, moe_w_down, ple_w_proj, ple_w_gate):
    w = dict(ln_g=ln_g, ln_b=ln_b, gla_w_in=gla_w_in, gla_w_a2=gla_w_a2, gla_b_a=gla_b_a,
             gla_norm_g=gla_norm_g, gla_w_o=gla_w_o, nsa_w_kv=nsa_w_kv,
             cmp_pe_k=cmp_pe_k, cmp_w1_k=cmp_w1_k, cmp_w2_k=cmp_w2_k,
             cmp_pe_v=cmp_pe_v, cmp_w1_v=cmp_w1_v, cmp_w2_v=cmp_w2_v,
             nsa_w_in=nsa_w_in, nsa_w_o=nsa_w_o, ffn_w_gate=ffn_w_gate, ffn_w_up=ffn_w_up,
             ffn_w_down=ffn_w_down, moe_w_router=moe_w_router, moe_w_gate=moe_w_gate,
             moe_w_up=moe_w_up, moe_w_down=moe_w_down, ple_w_proj=ple_w_proj, ple_w_gate=ple_w_gate)
    p = _prep_weights(w)
    Bp = x_prompt.shape[0]
    gla_zero = jnp.zeros((Bp, GLA_HEADS, GLA_DK, GLA_DV), F32)
    yp, gp, cp, sp, wp = _trunk(x_prompt, p_prompt, 0, gla_zero, None, p)
    past_len = page_table.shape[1] * PAGE_SIZE
    ys, gs, cs, ss, ws = _trunk(x_sample, p_sample, past_len, state_gla[0],
                                (cache_cmp, cache_sel, page_table, state_win), p)
    return (yp, ys, gp, gs, cp, cs, sp, ss, wp, ws)
```

```python
import functools
import math

import jax
import jax.numpy as jnp
from jax import lax
from jax.experimental import pallas as pl
from jax.experimental.pallas import tpu as pltpu

F32 = jnp.float32
BF16 = jnp.bfloat16

D_MODEL = 1024
DEPTH = 2
PAGE_SIZE = 128
PLE_DIM = 256
DEEPNORM_ALPHA = (2.0 * DEPTH) ** 0.25
LN_EPS = 1e-5
ROPE_THETA = 10000.0

GLA_HEADS = 4
GLA_DK = 128
GLA_DV = 256
GLA_GATE_RANK = 16
GLA_TAU = 16.0
GLA_CHUNK = 64

NSA_HEADS = 16
NSA_HD = 64
NSA_GROUPS = 2
NSA_HPG = 8
CMP_STRIDE = 16
CMP_LEN = 32
CMP_HIDDEN = 128
SEL_BLOCK = 64
SEL_TOP = 16
WINDOW = 512
NEG_INF = -1e30
FORCE_SCORE = 1e9

N_EXPERTS = 8
LANES = 128
VMEM_LIMIT = 56 * 1024 * 1024

_NT = (((1,), (1,)), ((), ()))
_TN = (((0,), (0,)), ((), ()))


def _cp(*sem):
    return pltpu.CompilerParams(dimension_semantics=sem, vmem_limit_bytes=VMEM_LIMIT)


def _dot(a, b):
    return jnp.dot(a, b, preferred_element_type=F32)


def _dot_nt(a, b):
    return lax.dot_general(a, b, _NT, preferred_element_type=F32)


def _dot_tn(a, b):
    return lax.dot_general(a, b, _TN, preferred_element_type=F32)


def _split2(x):
    hi = x.astype(BF16)
    lo = (x - hi.astype(F32)).astype(BF16)
    return hi, lo


def _split3(x):
    hi = x.astype(BF16)
    r = x - hi.astype(F32)
    mid = r.astype(BF16)
    lo = (r - mid.astype(F32)).astype(BF16)
    return hi, mid, lo


def _vmod(x, n):
    assert n & (n - 1) == 0
    return x & (n - 1)


def _vdiv(x, n):
    assert n & (n - 1) == 0
    return x >> (n.bit_length() - 1)


def _layer_norm(y, g, b):
    mu = jnp.mean(y, axis=-1, keepdims=True)
    yc = y - mu
    var = jnp.mean(yc * yc, axis=-1, keepdims=True)
    return yc * lax.rsqrt(var + LN_EPS) * g + b


def _rope_lanes(x, cos, sin):
    lane = lax.broadcasted_iota(jnp.int32, x.shape, x.ndim - 1)
    first = _vmod(lane, NSA_HD) < (NSA_HD // 2)
    partner = jnp.where(first, pltpu.roll(x, LANES - NSA_HD // 2, x.ndim - 1),
                        pltpu.roll(x, NSA_HD // 2, x.ndim - 1))
    return x * cos + partner * sin


def _proj_body(x_ref, w_ref, o_ref):
    o_ref[...] = _dot(x_ref[...], w_ref[...]).astype(o_ref.dtype)


def _proj(x, w, tm, tn, out_dtype):
    M, K = x.shape
    N = w.shape[1]
    return pl.pallas_call(
        _proj_body, grid=(M // tm, N // tn),
        in_specs=[pl.BlockSpec((tm, K), lambda i, j: (i, 0)),
                  pl.BlockSpec((K, tn), lambda i, j: (0, j))],
        out_specs=pl.BlockSpec((tm, tn), lambda i, j: (i, j)),
        out_shape=jax.ShapeDtypeStruct((M, N), out_dtype),
        compiler_params=_cp("parallel", "arbitrary"), name="proj")(x, w)


def _gla_gate_body(x_ref, wa_ref, wa2_ref, ba_ref, o_ref):
    a = _dot(x_ref[...], wa_ref[...])
    z = _dot(a.astype(BF16), wa2_ref[...]) + ba_ref[...]
    o_ref[...] = (jnp.minimum(z, 0.0) - jnp.log1p(jnp.exp(-jnp.abs(z)))) * (1.0 / GLA_TAU)


def _gla_gate(xb, wa, wa2, ba, tm):
    M = xb.shape[0]
    N = wa2.shape[1]
    return pl.pallas_call(
        _gla_gate_body, grid=(M // tm,),
        in_specs=[pl.BlockSpec((tm, D_MODEL), lambda i: (i, 0)),
                  pl.BlockSpec((D_MODEL, LANES), lambda i: (0, 0)),
                  pl.BlockSpec((LANES, N), lambda i: (0, 0)),
                  pl.BlockSpec((1, N), lambda i: (0, 0))],
        out_specs=pl.BlockSpec((tm, N), lambda i: (i, 0)),
        out_shape=jax.ShapeDtypeStruct((M, N), F32),
        compiler_params=_cp("parallel"), name="gla_gate")(xb, wa, wa2, ba)


def _gla_body(q_ref, k_ref, v_ref, r_ref, la_ref, g_ref, s0_ref, o_ref, sout_ref, s_ref, *, C, NC):
    t = pl.program_id(1)

    @pl.when(t == 0)
    def _():
        s_ref[...] = s0_ref[0]

    row = lax.broadcasted_iota(jnp.int32, (C, C), 0)
    col = lax.broadcasted_iota(jnp.int32, (C, C), 1)
    tri = col <= row
    ones = jnp.ones((C, LANES), BF16)
    rowk = lax.broadcasted_iota(jnp.int32, (C, GLA_DK), 0)
    rt = lax.broadcasted_iota(jnp.int32, (C * NC, C * NC), 0)
    ct = lax.broadcasted_iota(jnp.int32, (C * NC, C * NC), 1)
    tri_bd = ((ct <= rt) & (_vdiv(rt, C) == _vdiv(ct, C))).astype(BF16)
    la_hi, la_mid, la_lo = _split3(la_ref[...])
    b_all = _dot(tri_bd, la_hi) + _dot(tri_bd, la_mid) + _dot(tri_bd, la_lo)
    for h in range(GLA_HEADS):
        kc = slice(h * GLA_DK, (h + 1) * GLA_DK)
        vc = slice(h * GLA_DV, (h + 1) * GLA_DV)
        state = s_ref[h]
        for c in range(NC):
            sl = pl.ds(c * C, C)
            rs = slice(c * C, (c + 1) * C)
            b = b_all[rs, kc]
            tot = (_dot_tn(la_hi[rs, kc], ones) + _dot_tn(la_mid[rs, kc], ones)
                   + _dot_tn(la_lo[rs, kc], ones))
            b_mid = jnp.sum(jnp.where(rowk == (C - 1) // 2, b, 0.0), axis=0, keepdims=True)
            b_last = jnp.sum(jnp.where(rowk == C - 1, b, 0.0), axis=0, keepdims=True)
            q = q_ref[sl, kc] * (GLA_DK ** -0.5)
            k = k_ref[sl, kc]
            v = v_ref[sl, vc].astype(BF16)
            qe = (q * jnp.exp(b - b_mid)).astype(BF16)
            ke = (k * jnp.exp(b_mid - b)).astype(BF16)
            a = jnp.where(tri, _dot_nt(qe, ke), 0.0)
            o = _dot(a.astype(BF16), v) + _dot((q * jnp.exp(b)).astype(BF16), state.astype(BF16))
            kd = (k * jnp.exp(b_last - b)).astype(BF16)
            state = jnp.exp(tot[:, 0:1]) * state + _dot_tn(kd, v)
            ms = jnp.mean(o * o, axis=-1, keepdims=True)
            rr = r_ref[sl, vc]
            o = o * lax.rsqrt(ms + LN_EPS) * g_ref[...] * (rr * jax.nn.sigmoid(rr))
            o_ref[sl, vc] = o.astype(o_ref.dtype)
        s_ref[h] = state

    @pl.when(t == pl.num_programs(1) - 1)
    def _():
        sout_ref[0] = s_ref[...]


def _gla(proj, la, norm_g, s0, B, T, C, NC):
    Tt = C * NC
    nT = T // Tt
    M = B * T
    hk, hv = GLA_HEADS * GLA_DK, GLA_HEADS * GLA_DV
    body = functools.partial(_gla_body, C=C, NC=NC)
    state_spec = pl.BlockSpec((1, GLA_HEADS, GLA_DK, GLA_DV), lambda b, t: (b, 0, 0, 0))
    return pl.pallas_call(
        body, grid=(B, nT),
        in_specs=[pl.BlockSpec((Tt, hk), lambda b, t: (b * nT + t, 0)),
                  pl.BlockSpec((Tt, hk), lambda b, t: (b * nT + t, 1)),
                  pl.BlockSpec((Tt, hv), lambda b, t: (b * nT + t, 2 * hk // hv)),
                  pl.BlockSpec((Tt, hv), lambda b, t: (b * nT + t, 2 * hk // hv + 1)),
                  pl.BlockSpec((Tt, hk), lambda b, t: (b * nT + t, 0)),
                  pl.BlockSpec((1, GLA_DV), lambda b, t: (0, 0)),
                  state_spec],
        out_specs=[pl.BlockSpec((Tt, hv), lambda b, t: (b * nT + t, 0)), state_spec],
        out_shape=[jax.ShapeDtypeStruct((M, hv), BF16),
                   jax.ShapeDtypeStruct((B, GLA_HEADS, GLA_DK, GLA_DV), F32)],
        scratch_shapes=[pltpu.VMEM((GLA_HEADS, GLA_DK, GLA_DV), F32)],
        compiler_params=_cp("parallel", "arbitrary"), name="gla")(
            proj, proj, proj, proj, la, norm_g, s0)


def _mm_ln_body(a_ref, w_ref, r_ref, g_ref, b_ref, of_ref, ob_ref):
    y = DEEPNORM_ALPHA * r_ref[...] + _dot(a_ref[...], w_ref[...])
    out = _layer_norm(y, g_ref[...], b_ref[...])
    of_ref[...] = out
    ob_ref[...] = out.astype(BF16)


def _mm_ln(a, w, resid, g, b, tm):
    M, K = a.shape
    N = w.shape[1]
    return pl.pallas_call(
        _mm_ln_body, grid=(M // tm,),
        in_specs=[pl.BlockSpec((tm, K), lambda i: (i, 0)),
                  pl.BlockSpec((K, N), lambda i: (0, 0)),
                  pl.BlockSpec((tm, N), lambda i: (i, 0)),
                  pl.BlockSpec((1, N), lambda i: (0, 0)),
                  pl.BlockSpec((1, N), lambda i: (0, 0))],
        out_specs=[pl.BlockSpec((tm, N), lambda i: (i, 0)),
                   pl.BlockSpec((tm, N), lambda i: (i, 0))],
        out_shape=[jax.ShapeDtypeStruct((M, N), F32), jax.ShapeDtypeStruct((M, N), BF16)],
        compiler_params=_cp("parallel"), name="mm_ln")(a, w, resid, g, b)


def _ffn_body(xb_ref, xf_ref, wg_ref, wu_ref, wd_ref, g_ref, b_ref, of_ref, ob_ref, acc_ref):
    j = pl.program_id(1)

    @pl.when(j == 0)
    def _():
        acc_ref[...] = jnp.zeros_like(acc_ref)

    x = xb_ref[...]
    hg = _dot(x, wg_ref[...])
    hu = _dot(x, wu_ref[...])
    h = (hg * jax.nn.sigmoid(hg)) * hu
    acc_ref[...] += _dot(h.astype(BF16), wd_ref[...])

    @pl.when(j == pl.num_programs(1) - 1)
    def _():
        out = _layer_norm(DEEPNORM_ALPHA * xf_ref[...] + acc_ref[...], g_ref[...], b_ref[...])
        of_ref[...] = out
        ob_ref[...] = out.astype(BF16)


def _ffn(xb, xf, wg, wu, wd, g, b, tm, tf):
    M = xb.shape[0]
    F = wg.shape[1]
    return pl.pallas_call(
        _ffn_body, grid=(M // tm, F // tf),
        in_specs=[pl.BlockSpec((tm, D_MODEL), lambda i, j: (i, 0)),
                  pl.BlockSpec((tm, D_MODEL), lambda i, j: (i, 0)),
                  pl.BlockSpec((D_MODEL, tf), lambda i, j: (0, j)),
                  pl.BlockSpec((D_MODEL, tf), lambda i, j: (0, j)),
                  pl.BlockSpec((tf, D_MODEL), lambda i, j: (j, 0)),
                  pl.BlockSpec((1, D_MODEL), lambda i, j: (0, 0)),
                  pl.BlockSpec((1, D_MODEL), lambda i, j: (0, 0))],
        out_specs=[pl.BlockSpec((tm, D_MODEL), lambda i, j: (i, 0)),
                   pl.BlockSpec((tm, D_MODEL), lambda i, j: (i, 0))],
        out_shape=[jax.ShapeDtypeStruct((M, D_MODEL), F32), jax.ShapeDtypeStruct((M, D_MODEL), BF16)],
        scratch_shapes=[pltpu.VMEM((tm, D_MODEL), F32)],
        compiler_params=_cp("parallel", "arbitrary"), name="ffn")(xb, xf, wg, wu, wd, g, b)


def _ple_body(xb_ref, xf_ref, pb_ref, wg_ref, wp_ref, of_ref, ob_ref):
    gate = jax.nn.sigmoid(_dot(xb_ref[...], wg_ref[...]))
    out = xf_ref[...] + gate * _dot(pb_ref[...], wp_ref[...])
    of_ref[...] = out
    ob_ref[...] = out.astype(BF16)


def _ple(xb, xf, pb, wg, wp, tm):
    M = xb.shape[0]
    return pl.pallas_call(
        _ple_body, grid=(M // tm,),
        in_specs=[pl.BlockSpec((tm, D_MODEL), lambda i: (i, 0)),
                  pl.BlockSpec((tm, D_MODEL), lambda i: (i, 0)),
                  pl.BlockSpec((tm, PLE_DIM), lambda i: (i, 0)),
                  pl.BlockSpec((D_MODEL, D_MODEL), lambda i: (0, 0)),
                  pl.BlockSpec((PLE_DIM, D_MODEL), lambda i: (0, 0))],
        out_specs=[pl.BlockSpec((tm, D_MODEL), lambda i: (i, 0)),
                   pl.BlockSpec((tm, D_MODEL), lambda i: (i, 0))],
        out_shape=[jax.ShapeDtypeStruct((M, D_MODEL), F32), jax.ShapeDtypeStruct((M, D_MODEL), BF16)],
        compiler_params=_cp("parallel"), name="ple")(xb, xf, pb, wg, wp)


def _router_body(x_ref, w_ref, info_ref, info_t_ref, cnt_ref):
    x = x_ref[...]
    xh, xl = _split2(x)
    w = w_ref[...]
    wh, wl = _split2(w)
    logits = _dot(xh, wh) + _dot(xl, wh) + _dot(xh, wl)
    lane = lax.broadcasted_iota(jnp.int32, logits.shape, 1)
    live = lane < N_EXPERTS
    lg = jnp.where(live, logits, -jnp.inf)
    m1 = jnp.max(lg, axis=-1, keepdims=True)
    i1 = jnp.min(jnp.where(lg == m1, lane, LANES), axis=-1, keepdims=True)
    lg2 = jnp.where(lane == i1, -jnp.inf, lg)
    m2 = jnp.max(lg2, axis=-1, keepdims=True)
    i2 = jnp.min(jnp.where(lg2 == m2, lane, LANES), axis=-1, keepdims=True)
    e2 = jnp.exp(m2 - m1)
    den = 1.0 + e2
    onehot = jnp.where((lane == i1) | (lane == i2), 1.0, 0.0)
    tb = x.shape[0]
    before = (lax.broadcasted_iota(jnp.int32, (tb, tb), 1) < lax.broadcasted_iota(jnp.int32, (tb, tb), 0))
    cum = _dot(before.astype(BF16), onehot.astype(BF16))
    r1 = jnp.sum(jnp.where(lane == i1, cum, 0.0), axis=-1, keepdims=True)
    r2 = jnp.sum(jnp.where(lane == i2, cum, 0.0), axis=-1, keepdims=True)
    cols = (i1.astype(F32), i2.astype(F32), 1.0 / den, e2 / den, r1, r2)
    info = jnp.zeros(logits.shape, F32)
    for c, val in enumerate(cols):
        info = jnp.where(lane == c, val, info)
    info_ref[...] = info
    info_t_ref[0] = info.T[0:8]
    cnt_ref[0] = jnp.broadcast_to(jnp.sum(onehot, axis=0, keepdims=True), (8, LANES))


def _router(xf, w_pad, tb):
    M = xf.shape[0]
    return pl.pallas_call(
        _router_body, grid=(M // tb,),
        in_specs=[pl.BlockSpec((tb, D_MODEL), lambda i: (i, 0)),
                  pl.BlockSpec((D_MODEL, LANES), lambda i: (0, 0))],
        out_specs=[pl.BlockSpec((tb, LANES), lambda i: (i, 0)),
                   pl.BlockSpec((1, 8, tb), lambda i: (i, 0, 0)),
                   pl.BlockSpec((1, 8, LANES), lambda i: (i, 0, 0))],
        out_shape=[jax.ShapeDtypeStruct((M, LANES), F32),
                   jax.ShapeDtypeStruct((M // tb, 8, tb), F32),
                   jax.ShapeDtypeStruct((M // tb, 8, LANES), F32)],
        compiler_params=_cp("parallel"), name="router")(xf, w_pad)


MOE_TILE = 64
MOE_GROUP = 8


def _moe_sizes(M, tb):
    nb = M // tb
    LT = 2 * tb // MOE_TILE + N_EXPERTS
    n_tiles = -(-(nb * LT) // MOE_GROUP) * MOE_GROUP + N_EXPERTS * MOE_GROUP
    return nb, LT, n_tiles, n_tiles // MOE_GROUP


def _moe_plan(counts, LT, n_tiles, n_groups):
    nt = (counts + MOE_TILE - 1) // MOE_TILE
    lend = jnp.cumsum(nt, axis=1)
    lstart = lend - nt
    lt = jnp.arange(LT, dtype=jnp.int32)
    used = lt[None, :] < lend[:, -1:]
    e_of = jnp.minimum(jnp.sum(lt[None, :, None] >= lend[:, None, :], axis=-1), N_EXPERTS - 1).astype(jnp.int32)
    k_in = lt[None, :] - jnp.take_along_axis(lstart, e_of, axis=1)
    tot = jnp.sum(nt, axis=0)
    tot_pad = (tot + MOE_GROUP - 1) // MOE_GROUP * MOE_GROUP
    gend = jnp.cumsum(tot_pad)
    goff = gend - tot_pad
    boff = jnp.cumsum(nt, axis=0) - nt
    dt = goff[e_of] + jnp.take_along_axis(boff, e_of, axis=1) + k_in
    g4 = jnp.arange(n_groups, dtype=jnp.int32) * MOE_GROUP
    nu = (gend[-1] // MOE_GROUP).astype(jnp.int32)
    g4 = jnp.minimum(g4, gend[-1] - MOE_GROUP)
    ge = jnp.sum(g4[:, None] >= gend[None, :], axis=-1).astype(jnp.int32)
    nb = counts.shape[0]
    ids = (jnp.arange(nb, dtype=jnp.int32)[:, None] * LT + lt[None, :])
    src = jnp.full((n_tiles + 1,), -1, jnp.int32).at[jnp.where(used, dt, n_tiles).reshape(-1)].set(ids.reshape(-1))
    src = src[:n_tiles].reshape(n_groups, MOE_GROUP)
    gt = jnp.where(src < 0, src[:, 0:1], src)
    return dict(le=jnp.where(used, e_of, -1).astype(jnp.int32),
                lr=(k_in * MOE_TILE).astype(jnp.int32),
                dt=jnp.where(used, dt, 0).astype(jnp.int32),
                gt=jnp.maximum(gt, 0).astype(jnp.int32), ge=ge, nu=nu.reshape(1))


def _moe_dispatch_body(le_ref, lr_ref, x_ref, info_t_ref, o_ref, *, LT):
    b = pl.program_id(0)
    it = info_t_ref[0]
    i1, i2, r1, r2 = it[0:1], it[1:2], it[4:5], it[5:6]
    sub = lax.broadcasted_iota(jnp.int32, (MOE_TILE, 1), 0)
    tiles = []
    for lt in range(LT):
        ef = le_ref[b, lt].astype(F32)
        slot = (lr_ref[b, lt] + sub).astype(F32)
        hit = ((i1 == ef) & (r1 == slot)) | ((i2 == ef) & (r2 == slot))
        tiles.append(jnp.where(hit, 1.0, 0.0).astype(BF16))
    o_ref[...] = _dot(jnp.concatenate(tiles, axis=0), x_ref[...]).astype(BF16)


def _moe_dispatch(xb, info_t, plan, tb):
    nb, LT = plan["le"].shape
    grid_spec = pltpu.PrefetchScalarGridSpec(
        num_scalar_prefetch=2, grid=(nb,),
        in_specs=[pl.BlockSpec((tb, D_MODEL), lambda b, le, lr: (b, 0)),
                  pl.BlockSpec((1, 8, tb), lambda b, le, lr: (b, 0, 0))],
        out_specs=pl.BlockSpec((LT * MOE_TILE, D_MODEL), lambda b, le, lr: (b, 0)))
    return pl.pallas_call(
        functools.partial(_moe_dispatch_body, LT=LT), grid_spec=grid_spec,
        out_shape=jax.ShapeDtypeStruct((nb * LT * MOE_TILE, D_MODEL), BF16),
        compiler_params=_cp("arbitrary"), name="moe_dispatch")(plan["le"], plan["lr"], xb, info_t)


def _moe_ffn_body(gt_ref, ge_ref, nu_ref, *refs):
    x_refs = refs[:MOE_GROUP]
    wg_ref, wu_ref, wd_ref, o_ref, x_ref, acc_ref = refs[MOE_GROUP:]
    j = pl.program_id(1)

    @pl.when(pl.program_id(0) < nu_ref[0])
    def _():
        @pl.when(j == 0)
        def _():
            acc_ref[...] = jnp.zeros_like(acc_ref)
            for k in range(MOE_GROUP):
                x_ref[k * MOE_TILE:(k + 1) * MOE_TILE, :] = x_refs[k][...]

        x = x_ref[...]
        hg = _dot(x, wg_ref[...])
        hu = _dot(x, wu_ref[...])
        h = (hg * jax.nn.sigmoid(hg)) * hu
        acc_ref[...] += _dot(h.astype(BF16), wd_ref[...])

        @pl.when(j == pl.num_programs(1) - 1)
        def _():
            o_ref[...] = acc_ref[...].astype(BF16)

    @pl.when((pl.program_id(0) >= nu_ref[0]) & (j == 0))
    def _():
        o_ref[...] = jnp.zeros_like(o_ref)


def _moe_ffn(xs, plan, wg, wu, wd, n_groups, tf):
    tm = MOE_TILE * MOE_GROUP
    F = wg.shape[2]
    nf = F // tf
    live = lambda g, nu: jnp.minimum(g, nu[0] - 1)
    col = lambda g, j, nu: jnp.where(g < nu[0], j, nf - 1)
    x_specs = [pl.BlockSpec((MOE_TILE, D_MODEL), functools.partial(
        lambda g, j, gt, ge, nu, k: (gt[live(g, nu), k], 0), k=k)) for k in range(MOE_GROUP)]
    grid_spec = pltpu.PrefetchScalarGridSpec(
        num_scalar_prefetch=3, grid=(n_groups, nf),
        in_specs=x_specs + [
            pl.BlockSpec((None, D_MODEL, tf), lambda g, j, gt, ge, nu: (ge[g], 0, col(g, j, nu))),
            pl.BlockSpec((None, D_MODEL, tf), lambda g, j, gt, ge, nu: (ge[g], 0, col(g, j, nu))),
            pl.BlockSpec((None, tf, D_MODEL), lambda g, j, gt, ge, nu: (ge[g], col(g, j, nu), 0))],
        out_specs=pl.BlockSpec((tm, D_MODEL), lambda g, j, gt, ge, nu: (g, 0)),
        scratch_shapes=[pltpu.VMEM((tm, D_MODEL), BF16), pltpu.VMEM((tm, D_MODEL), F32)])
    return pl.pallas_call(
        _moe_ffn_body, grid_spec=grid_spec,
        out_shape=jax.ShapeDtypeStruct((n_groups * tm, D_MODEL), BF16),
        compiler_params=_cp("arbitrary", "arbitrary"), name="moe_ffn")(
            plan["gt"], plan["ge"], plan["nu"], *([xs] * MOE_GROUP), wg, wu, wd)


def _moe_combine_body(le_ref, lr_ref, dt_ref, *refs, LT, KC):
    y_refs = refs[:LT]
    info_ref, xf_ref, g_ref, b_ref, of_ref, ob_ref = refs[LT:]
    b = pl.program_id(0)
    info = info_ref[...]
    i1, i2, w1, w2, r1, r2 = (info[:, c:c + 1] for c in range(6))
    per = LANES // MOE_TILE
    lane = lax.broadcasted_iota(jnp.int32, (1, LANES), 1)
    moe = None
    for c0 in range(0, LT, KC):
        pw, ys = [], []
        for s0 in range(c0, c0 + KC, per):
            ef = jnp.zeros((1, LANES), F32)
            slot = jnp.zeros((1, LANES), F32)
            for k in range(per):
                here = (lane >= k * MOE_TILE) & (lane < (k + 1) * MOE_TILE)
                ef = jnp.where(here, le_ref[b, s0 + k].astype(F32), ef)
                slot = jnp.where(here, (lr_ref[b, s0 + k] - k * MOE_TILE + lane).astype(F32), slot)
            pw.append((jnp.where((i1 == ef) & (r1 == slot), w1, 0.0)
                       + jnp.where((i2 == ef) & (r2 == slot), w2, 0.0)).astype(BF16))
            ys.extend(y_refs[s0 + k][...] for k in range(per))
        part = _dot(jnp.concatenate(pw, axis=1), jnp.concatenate(ys, axis=0))
        moe = part if moe is None else moe + part
    out = _layer_norm(DEEPNORM_ALPHA * xf_ref[...] + moe, g_ref[...], b_ref[...])
    of_ref[...] = out
    ob_ref[...] = out.astype(BF16)


def _moe_combine(ys, info, xf, plan, g, b, tb):
    M = xf.shape[0]
    nb, LT = plan["le"].shape
    KC = 8 if LT % 8 == 0 else LT
    blk = lambda bb, le, lr, dt: (bb, 0)
    const = lambda bb, le, lr, dt: (0, 0)
    y_specs = [pl.BlockSpec((MOE_TILE, D_MODEL), functools.partial(
        lambda bb, le, lr, dt, k: (dt[bb, k], 0), k=k)) for k in range(LT)]
    grid_spec = pltpu.PrefetchScalarGridSpec(
        num_scalar_prefetch=3, grid=(nb,),
        in_specs=y_specs + [pl.BlockSpec((tb, LANES), blk),
                            pl.BlockSpec((tb, D_MODEL), blk),
                            pl.BlockSpec((1, D_MODEL), const),
                            pl.BlockSpec((1, D_MODEL), const)],
        out_specs=[pl.BlockSpec((tb, D_MODEL), blk), pl.BlockSpec((tb, D_MODEL), blk)])
    return pl.pallas_call(
        functools.partial(_moe_combine_body, LT=LT, KC=KC), grid_spec=grid_spec,
        out_shape=[jax.ShapeDtypeStruct((M, D_MODEL), F32), jax.ShapeDtypeStruct((M, D_MODEL), BF16)],
        compiler_params=_cp("arbitrary"), name="moe_combine")(
            plan["le"], plan["lr"], plan["dt"], *([ys] * LT), info, xf, g, b)


def _nsa_kv_body(s_ref, w_ref, cos_ref, sin_ref, cmp_ref, *rest, tm, nT, with_kvx):
    rows = _dot(s_ref[...], w_ref[...])
    cos = cos_ref[...]
    sin = sin_ref[...]
    cmp_ref[...] = rows[:, 0:256]
    ks = _rope_lanes(rows[:, 256:384], cos, sin)
    vs = rows[:, 384:512]
    kw = _rope_lanes(rows[:, 512:640], cos, sin)
    vw = rows[:, 640:768]
    if not with_kvx:
        sel_ref, win_ref = rest
        sel_ref[:, 0:128] = ks
        sel_ref[:, 128:256] = vs
        win_ref[:, 0:128] = kw
        win_ref[:, 128:256] = vw
    else:
        kvx_ref, cmp_t_ref, sel_t_ref, win_t_ref = rest
        cmp_t_ref[0, 0] = rows[:, 0:128].T
        cmp_t_ref[0, 1] = rows[:, 128:256].T
        sel_t_ref[0, 0] = ks.T
        sel_t_ref[0, 1] = vs.T
        win_t_ref[0, 0] = kw.T
        win_t_ref[0, 1] = vw.T
        lane = lax.broadcasted_iota(jnp.int32, (tm, LANES), 1)
        pos = (pl.program_id(0) % nT) * tm + lax.broadcasted_iota(jnp.int32, (tm, LANES), 0)
        low = lane < NSA_HD
        onehot = jnp.where(lane - NSA_HD == _vdiv(pos, SEL_BLOCK), 1.0, 0.0)
        for g in range(NSA_GROUPS):
            pick = (lambda x: x) if g == 0 else (lambda x: pltpu.roll(x, NSA_HD, 1))
            kvx_ref[g, :, 0:128] = jnp.where(low, pick(ks), onehot).astype(BF16)
            kvx_ref[g, :, 128:256] = jnp.where(low, pick(vs), 0.0).astype(BF16)
            kvx_ref[g, :, 256:384] = jnp.where(low, pick(kw), 0.0).astype(BF16)
            kvx_ref[g, :, 384:512] = jnp.where(low, pick(vw), 0.0).astype(BF16)


def _nsa_kv(sb, w, cos, sin, tm, with_kvx):
    M = sb.shape[0]
    nT = cos.shape[0] // tm
    body = functools.partial(_nsa_kv_body, tm=tm, nT=nT, with_kvx=with_kvx)
    if with_kvx:
        T = cos.shape[0]
        t_spec = pl.BlockSpec((1, 2, LANES, tm), lambda i: (i // nT, 0, 0, i % nT))
        t_shape = jax.ShapeDtypeStruct((M // T, 2, LANES, T), F32)
        out_specs = [pl.BlockSpec((tm, 256), lambda i: (i, 0)),
                     pl.BlockSpec((NSA_GROUPS, tm, 512), lambda i: (0, i, 0)), t_spec, t_spec, t_spec]
        out_shape = [jax.ShapeDtypeStruct((M, 256), F32),
                     jax.ShapeDtypeStruct((NSA_GROUPS, M, 512), BF16), t_shape, t_shape, t_shape]
    else:
        out_specs = [pl.BlockSpec((tm, 256), lambda i: (i, 0))] * 3
        out_shape = [jax.ShapeDtypeStruct((M, 256), F32)] * 3
    return pl.pallas_call(
        body, grid=(M // tm,),
        in_specs=[pl.BlockSpec((tm, D_MODEL), lambda i: (i, 0)),
                  pl.BlockSpec((D_MODEL, 768), lambda i: (0, 0)),
                  pl.BlockSpec((tm, LANES), lambda i: (i % nT, 0)),
                  pl.BlockSpec((tm, LANES), lambda i: (i % nT, 0))],
        out_specs=out_specs, out_shape=out_shape,
        compiler_params=_cp("parallel"), name="nsa_kv")(sb, w, cos, sin)


def _nsa_q_body(x_ref, w_ref, cos_ref, sin_ref, *rest):
    x = x_ref[...]
    res = _dot(x, w_ref[...])
    cos = cos_ref[...]
    sin = sin_ref[...]
    o_ref = rest[-1] if len(rest) == 1 else rest[1]
    for h in range(NSA_HEADS):
        xh = res[:, h * LANES:(h + 1) * LANES]
        o_ref[h] = (_rope_lanes(xh, cos, sin) * (NSA_HD ** -0.5)).astype(o_ref.dtype)
    if len(rest) == 3:
        wg_ref, _, g_ref = rest
        gates = jax.nn.sigmoid(_dot(x, wg_ref[...]))
        for g in range(NSA_GROUPS):
            g_ref[g] = gates[:, g * LANES:(g + 1) * LANES]


def _nsa_q(xb, w_pad, cos, sin, tm, out_dtype, w_gate=None):
    M = xb.shape[0]
    nT = cos.shape[0] // tm
    in_specs = [pl.BlockSpec((tm, D_MODEL), lambda i: (i, 0)),
                pl.BlockSpec((D_MODEL, NSA_HEADS * LANES), lambda i: (0, 0)),
                pl.BlockSpec((tm, LANES), lambda i: (i % nT, 0)),
                pl.BlockSpec((tm, LANES), lambda i: (i % nT, 0))]
    out_specs = [pl.BlockSpec((NSA_HEADS, tm, LANES), lambda i: (0, i, 0))]
    out_shape = [jax.ShapeDtypeStruct((NSA_HEADS, M, LANES), out_dtype)]
    args = [xb, w_pad, cos, sin]
    if w_gate is not None:
        in_specs.append(pl.BlockSpec((D_MODEL, NSA_GROUPS * LANES), lambda i: (0, 0)))
        out_specs.append(pl.BlockSpec((NSA_GROUPS, tm, LANES), lambda i: (0, i, 0)))
        out_shape.append(jax.ShapeDtypeStruct((NSA_GROUPS, M, LANES), F32))
        args.append(w_gate)
    return pl.pallas_call(
        _nsa_q_body, grid=(M // tm,), in_specs=in_specs, out_specs=out_specs, out_shape=out_shape,
        compiler_params=_cp("parallel"), name="nsa_q")(*args)


def _nsa_gate_body(x_ref, w_ref, o_ref):
    o_ref[0] = jax.nn.sigmoid(_dot(x_ref[...], w_ref[0]))


def _nsa_gate(xb, w_pad, tm):
    M = xb.shape[0]
    S = w_pad.shape[0]
    return pl.pallas_call(
        _nsa_gate_body, grid=(M // tm, S),
        in_specs=[pl.BlockSpec((tm, D_MODEL), lambda i, s: (i, 0)),
                  pl.BlockSpec((1, D_MODEL, LANES), lambda i, s: (s, 0, 0))],
        out_specs=pl.BlockSpec((1, tm, LANES), lambda i, s: (s, i, 0)),
        out_shape=jax.ShapeDtypeStruct((S, M, LANES), F32),
        compiler_params=_cp("parallel", "arbitrary"), name="nsa_gate")(xb, w_pad)


def _compress_rows(load, n_rows, w1_refs, w2_refs, pe_refs):
    lane = lax.broadcasted_iota(jnp.int32, (n_rows, LANES), 1)
    low = lane < NSA_HD
    out = []
    for kv in range(2):
        w1 = w1_refs[kv][...]
        bias = (_dot(pe_refs[kv][0], w1)[0:1, 0:CMP_HIDDEN]
                + _dot(pe_refs[kv][1], w1)[0:1, CMP_HIDDEN:2 * CMP_HIDDEN])
        g0, g1 = [], []
        for rp in range(CMP_STRIDE // 2):
            p = load(kv, 2 * rp)
            q = load(kv, 2 * rp + 1)
            g0.append(jnp.where(low, p, pltpu.roll(q, NSA_HD, 1)).astype(BF16))
            g1.append(jnp.where(low, pltpu.roll(p, NSA_HD, 1), q).astype(BF16))
        per_g = []
        for cols in (g0, g1):
            ab = _dot(jnp.concatenate(cols, axis=1), w1)
            nxt = pltpu.roll(ab[:, CMP_HIDDEN:2 * CMP_HIDDEN], n_rows - 1, 0)
            h = ab[:, 0:CMP_HIDDEN] + nxt + bias
            per_g.append(_dot(jax.nn.gelu(h).astype(BF16), w2_refs[kv][...]))
        out.append(per_g)
    return out


def _compress_prompt_body(x_ref, w1k, w1v, w2k, w2v, pek, pev, cos_ref, sin_ref, o_ref, *, n_rows):
    load = lambda kv, r: x_ref[:, r * 256 + kv * LANES:r * 256 + (kv + 1) * LANES]
    c = _compress_rows(load, n_rows, (w1k, w1v), (w2k, w2v), (pek, pev))
    for g in range(NSA_GROUPS):
        o_ref[0, g, 0] = _rope_lanes(c[0][g], cos_ref[...], sin_ref[...]).astype(BF16)
        o_ref[0, g, 1] = c[1][g].astype(BF16)


def _compress_prompt(xflat, cw, cos, sin, B, n_rows):
    body = functools.partial(_compress_prompt_body, n_rows=n_rows)
    const2 = lambda b: (0, 0)
    const3 = lambda b: (0, 0, 0)
    return pl.pallas_call(
        body, grid=(B,),
        in_specs=[pl.BlockSpec((n_rows, 4096), lambda b: (b, 0)),
                  pl.BlockSpec((1024, 256), const2), pl.BlockSpec((1024, 256), const2),
                  pl.BlockSpec((CMP_HIDDEN, LANES), const2), pl.BlockSpec((CMP_HIDDEN, LANES), const2),
                  pl.BlockSpec((2, 16, 1024), const3), pl.BlockSpec((2, 16, 1024), const3),
                  pl.BlockSpec((n_rows, LANES), const2), pl.BlockSpec((n_rows, LANES), const2)],
        out_specs=pl.BlockSpec((1, NSA_GROUPS, 2, n_rows, LANES), lambda b: (b, 0, 0, 0, 0)),
        out_shape=jax.ShapeDtypeStruct((B, NSA_GROUPS, 2, n_rows, LANES), BF16),
        compiler_params=_cp("parallel"), name="compress_prompt")(
            xflat, cw["w1k"], cw["w1v"], cw["w2k"], cw["w2v"], cw["pek"], cw["pev"], cos, sin)


def _compress_sample_body(pt_ref, *refs, n_rows, PG):
    pages = refs[:PG]
    w1k, w1v, w2k, w2v, pek, pev, cos_ref, sin_ref, o_ref, x_ref = refs[PG:]
    j = pl.program_id(1)
    cpp = PAGE_SIZE // CMP_STRIDE
    orow = lax.broadcasted_iota(jnp.int32, (PAGE_SIZE, PAGE_SIZE), 0)
    pcol = lax.broadcasted_iota(jnp.int32, (PAGE_SIZE, PAGE_SIZE), 1)
    perm = (pcol == _vmod(orow, cpp) * CMP_STRIDE + _vdiv(orow, cpp)).astype(BF16)
    for k in range(PG):
        c0 = pl.multiple_of((j * PG + k) * cpp, cpp)
        for kv in range(2):
            xp = _dot_nt(perm, pages[k][0, kv].astype(BF16))
            for r in range(CMP_STRIDE):
                x_ref[kv, r, pl.ds(c0, cpp), :] = xp[r * cpp:(r + 1) * cpp, :]

    @pl.when(j == pl.num_programs(1) - 1)
    def _():
        load = lambda kv, r: x_ref[kv, r]
        c = _compress_rows(load, n_rows, (w1k, w1v), (w2k, w2v), (pek, pev))
        ck = c[0][0] + pltpu.roll(c[0][1], NSA_HD, 1)
        cv = c[1][0] + pltpu.roll(c[1][1], NSA_HD, 1)
        o_ref[0, :, 0:128] = _rope_lanes(ck, cos_ref[...], sin_ref[...]).astype(BF16)
        o_ref[0, :, 128:256] = cv.astype(BF16)


def _compress_sample(cache_t, page_table, cw, cos, sin, PG):
    B, n_pages = page_table.shape
    n_rows = n_pages * PAGE_SIZE // CMP_STRIDE
    body = functools.partial(_compress_sample_body, n_rows=n_rows, PG=PG)
    const2 = lambda b, j, pt: (0, 0)
    const3 = lambda b, j, pt: (0, 0, 0)
    page_specs = [pl.BlockSpec((1, 2, LANES, PAGE_SIZE), functools.partial(
        lambda b, j, pt, k: (pt[b, j * PG + k], 0, 0, 0), k=k)) for k in range(PG)]
    grid_spec = pltpu.PrefetchScalarGridSpec(
        num_scalar_prefetch=1, grid=(B, n_pages // PG),
        in_specs=page_specs + [
            pl.BlockSpec((1024, 256), const2), pl.BlockSpec((1024, 256), const2),
            pl.BlockSpec((CMP_HIDDEN, LANES), const2), pl.BlockSpec((CMP_HIDDEN, LANES), const2),
            pl.BlockSpec((2, 16, 1024), const3), pl.BlockSpec((2, 16, 1024), const3),
            pl.BlockSpec((n_rows, LANES), const2), pl.BlockSpec((n_rows, LANES), const2)],
        out_specs=pl.BlockSpec((1, n_rows, 256), lambda b, j, pt: (b, 0, 0)),
        scratch_shapes=[pltpu.VMEM((2, CMP_STRIDE, n_rows, LANES), F32)])
    return pl.pallas_call(
        body, grid_spec=grid_spec,
        out_shape=jax.ShapeDtypeStruct((B, n_rows, 256), BF16),
        compiler_params=_cp("parallel", "arbitrary"), name="compress_sample")(
            page_table, *([cache_t] * PG), cw["w1k"], cw["w1v"], cw["w2k_lo"], cw["w2v_lo"],
            cw["pek"], cw["pev"], cos, sin)


def _nsa_attn_body(q_ref, gate_ref, ckv_ref, ovt_ref, kvx_ref, o_ref,
                   s_ref, p_ref, m_ref, l_ref, a_ref, acc_ref, *, Tq, Tk, n_cmp, n_sel):
    q0 = pl.program_id(2) * Tq
    R = NSA_HPG * Tq
    NB = 32
    q3 = q_ref[...]
    qh = q3.reshape(R, LANES)
    tq_lane = q0 + lax.broadcasted_iota(jnp.int32, (1, Tq), 1)
    t_lane = jnp.concatenate([tq_lane] * NSA_HPG, axis=1)

    ck = ckv_ref[0, 0, 0]
    cv = ckv_ref[0, 0, 1]
    n_rows = ck.shape[0]
    s = _dot_nt(ck, qh)
    nidx = lax.broadcasted_iota(jnp.int32, (n_rows, 1), 0)
    cmask = (nidx < n_cmp) & (CMP_STRIDE * nidx + (CMP_LEN - 1) <= t_lane)
    s = jnp.where(cmask, s, NEG_INF)
    e = jnp.where(cmask, jnp.exp(s - jnp.max(s, axis=0, keepdims=True)), 0.0)
    p = e / jnp.maximum(jnp.sum(e, axis=0, keepdims=True), 1e-30)
    o_cmp = _dot_tn(cv, p.astype(BF16))
    psum = p[:, 0:Tq]
    for h in range(1, NSA_HPG):
        psum = psum + p[:, h * Tq:(h + 1) * Tq]
    ph, plo = _split2(psum)
    imp = _dot(ovt_ref[...], ph) + _dot(ovt_ref[...], plo)

    sc = imp[NSA_HD:NSA_HD + NB, :]
    jj = lax.broadcasted_iota(jnp.int32, (NB, 1), 0)
    cur = _vdiv(tq_lane, SEL_BLOCK)
    forced = (jj == 0) | (jj == cur) | (jj == cur - 1)
    valid = SEL_BLOCK * jj <= tq_lane
    sc = jnp.where(valid, jnp.where(forced, FORCE_SCORE, sc), -FORCE_SCORE)
    rank = jnp.zeros((NB, Tq), F32)
    for jp in range(n_sel):
        c = sc[jp:jp + 1, :]
        beats = (c > sc) | ((c == sc) & (jp < jj))
        rank = rank + jnp.where(beats, 1.0, 0.0)
    drop = (rank >= float(min(SEL_TOP, n_sel))) & (jj < n_sel)
    bias_t = jnp.concatenate([jnp.zeros((NSA_HD, Tq), F32), jnp.where(drop, NEG_INF, 0.0),
                              jnp.zeros((LANES - NSA_HD - NB, Tq), F32)], axis=0)
    qa = (q3.astype(F32) + bias_t.T[None]).astype(BF16).reshape(R, LANES)

    def reset():
        m_ref[...] = jnp.full(m_ref.shape, NEG_INF, F32)
        l_ref[...] = jnp.zeros(l_ref.shape, F32)
        acc_ref[...] = jnp.zeros(acc_ref.shape, F32)

    def flash_tile(kt, tk, qv, kcol, masked, window):
        k0 = pl.multiple_of(kt * tk, tk)
        ka = kvx_ref[0, pl.ds(k0, tk), kcol:kcol + LANES]
        va = kvx_ref[0, pl.ds(k0, tk), kcol + LANES:kcol + 2 * LANES]
        s_ref[0:tk, :] = _dot_nt(ka, qv)
        if masked:
            kpos = k0 + lax.broadcasted_iota(jnp.int32, (tk, 1), 0)
            ok = kpos <= tq_lane
            if window:
                ok = ok & (kpos > tq_lane - WINDOW)
        for h in range(NSA_HPG):
            cols = slice(h * Tq, (h + 1) * Tq)
            s = s_ref[0:tk, cols]
            if masked:
                s = jnp.where(ok, s, NEG_INF)
            m_old = m_ref[0:1, cols]
            m_new = jnp.maximum(m_old, jnp.max(s, axis=0, keepdims=True))
            a = jnp.exp(m_old - m_new)
            p = jnp.exp(s - m_new)
            l_ref[0:1, cols] = a * l_ref[0:1, cols] + jnp.sum(p, axis=0, keepdims=True)
            m_ref[0:1, cols] = m_new
            a_ref[0:1, cols] = a
            p_ref[0:tk, cols] = p.astype(BF16)
        acc_ref[...] = a_ref[0:1, :] * acc_ref[...] + _dot_tn(va, p_ref[0:tk, :])

    def run(lo, hi, tk, qv, kcol, masked, window):
        def step(kt, carry):
            flash_tile(kt, tk, qv, kcol, masked, window)
            return carry
        lax.fori_loop(lo, hi, step, 0)

    Th = Tk // 2
    kd = q0 // Tk
    reset()
    run(0, kd, Tk, qa, 0, False, False)
    run(2 * kd, (q0 + Tq + Th - 1) // Th, Th, qa, 0, True, False)
    o_sel = acc_ref[...] / l_ref[0:1, :]

    reset()
    run(jnp.maximum(q0 - WINDOW + 1, 0) // Th, (q0 + Tq + Th - 1) // Th, Th, qh, 2 * LANES, True, True)
    o_win = acc_ref[...] / l_ref[0:1, :]

    gates_t = gate_ref[0].T
    heads = []
    for h in range(NSA_HPG):
        cols = slice(h * Tq, (h + 1) * Tq)
        o_t = (gates_t[h:h + 1, :] * o_cmp[:, cols]
               + gates_t[NSA_HPG + h:NSA_HPG + h + 1, :] * o_sel[:, cols]
               + gates_t[2 * NSA_HPG + h:2 * NSA_HPG + h + 1, :] * o_win[:, cols])
        heads.append(o_t.T)
    low = lax.broadcasted_iota(jnp.int32, (1, LANES), 1) < NSA_HD
    for hp in range(NSA_HPG // 2):
        pair = jnp.where(low, heads[2 * hp], pltpu.roll(heads[2 * hp + 1], NSA_HD, 1))
        o_ref[:, hp * LANES:(hp + 1) * LANES] = pair.astype(o_ref.dtype)


def _nsa_attn_prompt(q, gates, ckv, ov, kvx, B, T, Tq, Tk, n_cmp, n_sel):
    nq = T // Tq
    M = B * T
    n_rows = ckv.shape[3]
    assert (Tk // 2) % Tq == 0 and Tq % LANES == 0 and n_sel <= 32
    R = NSA_HPG * Tq
    body = functools.partial(_nsa_attn_body, Tq=Tq, Tk=Tk, n_cmp=n_cmp, n_sel=n_sel)
    scratch = [pltpu.VMEM((Tk, R), F32), pltpu.VMEM((Tk, R), BF16), pltpu.VMEM((8, R), F32),
               pltpu.VMEM((8, R), F32), pltpu.VMEM((8, R), F32), pltpu.VMEM((LANES, R), F32)]
    return pl.pallas_call(
        body, grid=(B, NSA_GROUPS, nq),
        in_specs=[pl.BlockSpec((NSA_HPG, Tq, LANES), lambda b, g, i: (g, b * nq + i, 0)),
                  pl.BlockSpec((1, Tq, LANES), lambda b, g, i: (g, b * nq + i, 0)),
                  pl.BlockSpec((1, 1, 2, n_rows, LANES), lambda b, g, i: (b, g, 0, 0, 0)),
                  pl.BlockSpec((LANES, n_rows), lambda b, g, i: (0, 0)),
                  pl.BlockSpec((1, T, 512), lambda b, g, i: (g, b, 0))],
        out_specs=pl.BlockSpec((Tq, NSA_HPG * NSA_HD), lambda b, g, i: (b * nq + i, g)),
        out_shape=jax.ShapeDtypeStruct((M, NSA_HEADS * NSA_HD), BF16),
        scratch_shapes=scratch,
        compiler_params=_cp("parallel", "parallel", "arbitrary"), name="nsa_attn_prompt")(
            q, gates, ckv, ov, kvx)


def _col_softmax_update(s, m, l, acc, v):
    m_new = jnp.maximum(m, jnp.max(s, axis=0, keepdims=True))
    a = jnp.exp(m - m_new)
    p = jnp.exp(s - m_new)
    l = a * l + jnp.sum(p, axis=0, keepdims=True)
    return m_new, l, a, _dot_tn(p.astype(BF16), v)


def _nsa_attn_sample_body(pt_ref, *refs, T, PG, past_len, n_cmp, n_sel, n_rows, wb):
    pages = refs[:PG]
    (q_ref, gate_ref, ckv_ref, ovt_ref, swin_ref, wnew_ref, snew_ref, o_ref,
     qb_ref, selb_ref, m_ref, l_ref, acc_ref, oc_ref, ow_ref) = refs[PG:]
    j = pl.program_id(1)
    NQ = NSA_HEADS * T
    lane = lax.broadcasted_iota(jnp.int32, (1, NQ), 1)
    tl = past_len + _vmod(lane, T)
    eye = (lax.broadcasted_iota(jnp.int32, (NQ, NQ), 0) == lax.broadcasted_iota(jnp.int32, (NQ, NQ), 1))

    def to_rows(x):
        return jnp.sum(jnp.where(eye, x, 0.0), axis=1, keepdims=True)

    @pl.when(j == 0)
    def _():
        qp = q_ref[...].reshape(NQ, LANES)
        rowi = lax.broadcasted_iota(jnp.int32, (NQ, LANES), 0)
        qb = jnp.where(rowi < NQ // 2, qp, pltpu.roll(qp, NSA_HD, 1)).astype(BF16)
        qb_ref[...] = qb

        ck = ckv_ref[0, :, 0:128]
        cv = ckv_ref[0, :, 128:256]
        s = _dot_nt(ck, qb)
        nidx = lax.broadcasted_iota(jnp.int32, (n_rows, 1), 0)
        cmask = (nidx < n_cmp) & (CMP_STRIDE * nidx + (CMP_LEN - 1) <= tl)
        s = jnp.where(cmask, s, NEG_INF)
        e = jnp.where(cmask, jnp.exp(s - jnp.max(s, axis=0, keepdims=True)), 0.0)
        p = e / jnp.maximum(jnp.sum(e, axis=0, keepdims=True), 1e-30)
        oc_ref[...] = _dot_tn(p.astype(BF16), cv)
        ri = lax.broadcasted_iota(jnp.int32, (NQ, NQ), 0)
        ci = lax.broadcasted_iota(jnp.int32, (NQ, NQ), 1)
        same = ((_vdiv(ri, NQ // 2) == _vdiv(ci, NQ // 2)) & (_vmod(ri, T) == _vmod(ci, T))).astype(BF16)
        ph, plo = _split2(p)
        psum = _dot(ph, same) + _dot(plo, same)
        sh, slo = _split2(psum)
        imp = _dot(ovt_ref[...], sh) + _dot(ovt_ref[...], slo)

        nb = imp.shape[0]
        jj = lax.broadcasted_iota(jnp.int32, (nb, 1), 0)
        cur = _vdiv(tl, SEL_BLOCK)
        forced = (jj == 0) | (jj == cur) | (jj == cur - 1)
        valid = SEL_BLOCK * jj <= tl
        score = jnp.where(valid, jnp.where(forced, FORCE_SCORE, imp), -FORCE_SCORE)
        work = jnp.where(jj < n_sel, score, -2e38)
        keep = jnp.zeros(work.shape, jnp.bool_)
        for _ in range(min(SEL_TOP, n_sel)):
            mx = jnp.max(work, axis=0, keepdims=True)
            first = jnp.min(jnp.where(work == mx, jj, nb), axis=0, keepdims=True)
            pick = jj == first
            keep = keep | pick
            work = jnp.where(pick, -3e38, work)
        selb_ref[...] = jnp.where(keep, 0.0, NEG_INF)

        kw = swin_ref[0, 0].T.astype(BF16)
        vw = swin_ref[0, 1].T.astype(BF16)
        sw = _dot_nt(kw, qb)
        kpos = past_len - wb + lax.broadcasted_iota(jnp.int32, (wb, 1), 0)
        sw = jnp.where((kpos <= tl) & (kpos > tl - WINDOW), sw, NEG_INF)
        zero = jnp.zeros((T, 256), F32)
        wn = jnp.concatenate([wnew_ref[...], zero], axis=0)
        sn = _dot_nt(wn[:, 0:128].astype(BF16), qb)
        kposn = past_len + lax.broadcasted_iota(jnp.int32, (2 * T, 1), 0)
        sn = jnp.where((kposn <= tl) & (kposn > tl - WINDOW), sn, NEG_INF)
        mw = jnp.maximum(jnp.max(sw, axis=0, keepdims=True), jnp.max(sn, axis=0, keepdims=True))
        pw = jnp.exp(sw - mw)
        pn = jnp.exp(sn - mw)
        lw = jnp.sum(pw, axis=0, keepdims=True) + jnp.sum(pn, axis=0, keepdims=True)
        ow = _dot_tn(pw.astype(BF16), vw) + _dot_tn(pn.astype(BF16), wn[:, 128:256].astype(BF16))
        ow_ref[...] = ow / to_rows(lw)

        m_ref[...] = jnp.full(m_ref.shape, NEG_INF, F32)
        l_ref[...] = jnp.zeros(l_ref.shape, F32)
        acc_ref[...] = jnp.zeros(acc_ref.shape, F32)

    qb = qb_ref[...]
    kt = jnp.concatenate([pg[0, 0].T.astype(BF16) for pg in pages], axis=0)
    vt = jnp.concatenate([pg[0, 1].T.astype(BF16) for pg in pages], axis=0)
    blk_per_step = PG * PAGE_SIZE // SEL_BLOCK
    sb = selb_ref[pl.ds(pl.multiple_of(j * blk_per_step, blk_per_step), blk_per_step), :]
    sb = jnp.broadcast_to(sb[:, None, :], (blk_per_step, SEL_BLOCK, NQ)).reshape(PG * PAGE_SIZE, NQ)
    sc = _dot_nt(kt, qb) + sb
    m_new, l_new, a, pv = _col_softmax_update(sc, m_ref[0:1, :], l_ref[0:1, :], None, vt)
    m_ref[0:1, :] = m_new
    l_ref[0:1, :] = l_new
    acc_ref[...] = to_rows(a) * acc_ref[...] + pv

    @pl.when(j == pl.num_programs(1) - 1)
    def _():
        zero = jnp.zeros((T, 256), F32)
        sn_rows = jnp.concatenate([snew_ref[...], zero], axis=0)
        kposn = past_len + lax.broadcasted_iota(jnp.int32, (2 * T, 1), 0)
        n_past_blk = past_len // SEL_BLOCK
        sbn = selb_ref[n_past_blk:n_past_blk + 1, :]
        sc = _dot_nt(sn_rows[:, 0:128].astype(BF16), qb) + sbn
        sc = jnp.where(kposn <= tl, sc, NEG_INF)
        m_new, l_new, a, pv = _col_softmax_update(sc, m_ref[0:1, :], l_ref[0:1, :], None,
                                                  sn_rows[:, 128:256].astype(BF16))
        o_sel = (to_rows(a) * acc_ref[...] + pv) / to_rows(l_new)
        gates = gate_ref[...].reshape(NQ, LANES)
        o = gates[:, 0:1] * oc_ref[...] + gates[:, 1:2] * o_sel + gates[:, 2:3] * ow_ref[...]
        low = lax.broadcasted_iota(jnp.int32, (T, LANES), 1) < NSA_HD
        for hp in range(NSA_HEADS // 2):
            a0 = o[(2 * hp) * T:(2 * hp + 1) * T]
            a1 = o[(2 * hp + 1) * T:(2 * hp + 2) * T]
            if 2 * hp < NSA_HPG:
                piece = jnp.where(low, a0, pltpu.roll(a1, NSA_HD, 1))
            else:
                piece = jnp.where(low, pltpu.roll(a0, NSA_HD, 1), a1)
            o_ref[:, hp * LANES:(hp + 1) * LANES] = piece


def _nsa_attn_sample(q, gates, ckv, ovt, cache_sel_t, page_table, state_win_t, win_new, sel_new, T, PG):
    B, n_pages = page_table.shape
    past_len = n_pages * PAGE_SIZE
    n_rows = ckv.shape[1]
    wb = state_win_t.shape[3]
    n_cmp = (past_len + T) // CMP_STRIDE - 1
    n_sel = past_len // SEL_BLOCK + -(-T // SEL_BLOCK)
    nb = ovt.shape[0]
    NQ = NSA_HEADS * T
    body = functools.partial(_nsa_attn_sample_body, T=T, PG=PG, past_len=past_len, n_cmp=n_cmp,
                             n_sel=n_sel, n_rows=n_rows, wb=wb)
    page_specs = [pl.BlockSpec((1, 2, LANES, PAGE_SIZE), functools.partial(
        lambda b, j, pt, k: (pt[b, j * PG + k], 0, 0, 0), k=k)) for k in range(PG)]
    grid_spec = pltpu.PrefetchScalarGridSpec(
        num_scalar_prefetch=1, grid=(B, n_pages // PG),
        in_specs=page_specs + [
            pl.BlockSpec((NSA_HEADS, T, LANES), lambda b, j, pt: (0, b, 0)),
            pl.BlockSpec((NSA_HEADS, T, LANES), lambda b, j, pt: (0, b, 0)),
            pl.BlockSpec((1, n_rows, 256), lambda b, j, pt: (b, 0, 0)),
            pl.BlockSpec((nb, n_rows), lambda b, j, pt: (0, 0)),
            pl.BlockSpec((1, 2, LANES, wb), lambda b, j, pt: (b, 0, 0, 0)),
            pl.BlockSpec((T, 256), lambda b, j, pt: (b, 0)),
            pl.BlockSpec((T, 256), lambda b, j, pt: (b, 0))],
        out_specs=pl.BlockSpec((T, NSA_HEADS * NSA_HD), lambda b, j, pt: (b, 0)),
        scratch_shapes=[pltpu.VMEM((NQ, LANES), BF16), pltpu.VMEM((nb, NQ), F32),
                        pltpu.VMEM((8, NQ), F32), pltpu.VMEM((8, NQ), F32),
                        pltpu.VMEM((NQ, LANES), F32), pltpu.VMEM((NQ, LANES), F32),
                        pltpu.VMEM((NQ, LANES), F32)])
    return pl.pallas_call(
        body, grid_spec=grid_spec,
        out_shape=jax.ShapeDtypeStruct((B * T, NSA_HEADS * NSA_HD), F32),
        compiler_params=_cp("parallel", "arbitrary"), name="nsa_attn_sample")(
            page_table, *([cache_sel_t] * PG), q, gates, ckv, ovt, state_win_t, win_new, sel_new)


def _rope_tables(pos, pattern):
    half = NSA_HD // 2
    inv = ROPE_THETA ** (-jnp.arange(half, dtype=F32) / half)
    ang = pos.astype(F32)[:, None] * inv[None, :]
    cos, sin = jnp.cos(ang), jnp.sin(ang)
    zero = jnp.zeros_like(cos)
    cs = [jnp.concatenate([cos, cos], 1) if p else jnp.concatenate([zero, zero], 1) for p in pattern]
    sn = [jnp.concatenate([-sin, sin], 1) if p else jnp.concatenate([zero, zero], 1) for p in pattern]
    return jnp.concatenate(cs, 1), jnp.concatenate(sn, 1)


def _overlap(n_rows, n_sel):
    ci = jnp.arange(n_rows, dtype=jnp.int32)[:, None] * CMP_STRIDE
    sj = jnp.arange(n_sel, dtype=jnp.int32)[None, :] * SEL_BLOCK
    return ((ci < sj + SEL_BLOCK) & (ci + CMP_LEN > sj)).astype(F32)


def _prep_weights(w):
    d = D_MODEL
    hk, hv = GLA_HEADS * GLA_DK, GLA_HEADS * GLA_DV
    p = {}
    p["ln_g"] = w["ln_g"].reshape(DEPTH, 2, 1, d)
    p["ln_b"] = w["ln_b"].reshape(DEPTH, 2, 1, d)
    w_in = w["gla_w_in"][0]
    p["gla_w_qkvr"] = w_in[:, :2 * hk + 2 * hv].astype(BF16)
    p["gla_w_a"] = jnp.pad(w_in[:, 2 * hk + 2 * hv:], ((0, 0), (0, LANES - GLA_GATE_RANK))).astype(BF16)
    p["gla_w_a2"] = jnp.pad(w["gla_w_a2"][0], ((0, LANES - GLA_GATE_RANK), (0, 0))).astype(BF16)
    p["gla_b_a"] = w["gla_b_a"][0].reshape(1, hk)
    p["gla_norm_g"] = w["gla_norm_g"][0].reshape(1, GLA_DV)
    p["gla_w_o"] = w["gla_w_o"][0].astype(BF16)
    p["nsa_w_kv"] = w["nsa_w_kv"].astype(BF16)
    nw = w["nsa_w_in"][0]
    wq = nw[:, :NSA_HEADS * NSA_HD].reshape(d, NSA_HEADS, NSA_HD)
    p["nsa_w_q"] = jnp.pad(wq, ((0, 0), (0, 0), (0, LANES - NSA_HD))).reshape(d, NSA_HEADS * LANES).astype(BF16)
    wg = nw[:, NSA_HEADS * NSA_HD:].reshape(d, 3, NSA_GROUPS, NSA_HPG)
    wg_grp = jnp.transpose(wg, (2, 0, 1, 3)).reshape(NSA_GROUPS, d, 3 * NSA_HPG)
    wg_grp = jnp.pad(wg_grp, ((0, 0), (0, 0), (0, LANES - 3 * NSA_HPG)))
    p["nsa_w_gate_grp"] = jnp.transpose(wg_grp, (1, 0, 2)).reshape(d, NSA_GROUPS * LANES).astype(BF16)
    wg_head = jnp.transpose(wg, (2, 3, 0, 1)).reshape(NSA_HEADS, d, 3)
    p["nsa_w_gate_head"] = jnp.pad(wg_head, ((0, 0), (0, 0), (0, LANES - 3))).astype(BF16)
    p["nsa_w_o"] = w["nsa_w_o"][0].astype(BF16)
    cw = {}
    for name, pe, w1, w2 in (("k", w["cmp_pe_k"], w["cmp_w1_k"], w["cmp_w2_k"]),
                             ("v", w["cmp_pe_v"], w["cmp_w1_v"], w["cmp_w2_v"])):
        half = CMP_STRIDE * NSA_HD
        cw["w1" + name] = jnp.concatenate([w1[:half], w1[half:]], axis=1).astype(BF16)
        cw["w2" + name] = jnp.pad(w2, ((0, 0), (0, LANES - NSA_HD))).astype(BF16)
        cw["w2" + name + "_lo"] = cw["w2" + name]
        pe2 = pe.reshape(2, 1, half)
        cw["pe" + name] = jnp.broadcast_to(pe2, (2, 16, half)).astype(BF16)
    p["cmp"] = cw
    p["ffn_w_gate"] = w["ffn_w_gate"][0].astype(BF16)
    p["ffn_w_up"] = w["ffn_w_up"][0].astype(BF16)
    p["ffn_w_down"] = w["ffn_w_down"][0].astype(BF16)
    p["moe_w_router"] = jnp.pad(w["moe_w_router"][0], ((0, 0), (0, LANES - N_EXPERTS)))
    p["moe_w_gate"] = w["moe_w_gate"][0].astype(BF16)
    p["moe_w_up"] = w["moe_w_up"][0].astype(BF16)
    p["moe_w_down"] = w["moe_w_down"][0].astype(BF16)
    p["ple_w_proj"] = w["ple_w_proj"].astype(BF16)
    p["ple_w_gate"] = w["ple_w_gate"].astype(BF16)
    return p


def _tile(M, cap):
    t = cap
    while M % t:
        t //= 2
    return t


def _trunk(x, ple, pos0, gla_s0, past, p):
    B, T, d = x.shape
    M = B * T
    tm = _tile(M, 1024)
    tm_s = _tile(M, 512)
    xf = x.reshape(M, d)
    xb = xf.astype(BF16)
    pleb = ple.reshape(DEPTH, M, PLE_DIM).astype(BF16)

    proj = _proj(xb, p["gla_w_qkvr"], tm, 512, F32)
    la = _gla_gate(xb, p["gla_w_a"], p["gla_w_a2"], p["gla_b_a"], tm)
    if T >= GLA_CHUNK:
        C, NC, Tp = GLA_CHUNK, _tile(T // GLA_CHUNK, 4), T
    else:
        C, NC, Tp = 16, 1, 16
        pad = lambda a: jnp.pad(a.reshape(B, T, -1), ((0, 0), (0, Tp - T), (0, 0))).reshape(B * Tp, -1)
        proj, la = pad(proj), pad(la)
    o, s_new = _gla(proj, la, p["gla_norm_g"], gla_s0, B, Tp, C, NC)
    if Tp != T:
        o = o.reshape(B, Tp, -1)[:, :T].reshape(M, -1)
    x1f, x1b = _mm_ln(o, p["gla_w_o"], xf, p["ln_g"][0, 0], p["ln_b"][0, 0], tm_s)
    x2f, x2b = _ffn(x1b, x1f, p["ffn_w_gate"], p["ffn_w_up"], p["ffn_w_down"],
                    p["ln_g"][0, 1], p["ln_b"][0, 1], tm_s, 1408)
    x3f, x3b = _ple(x2b, x2f, pleb[0], p["ple_w_gate"][0], p["ple_w_proj"][0], tm_s)

    qpos = pos0 + jnp.arange(T, dtype=jnp.int32)
    cos_kv, sin_kv = _rope_tables(qpos, (1, 1))
    cos_q, sin_q = _rope_tables(qpos, (1, 0))
    if past is None:
        tq = _tile(T, 512)
    else:
        tq = _tile(M, 512)
        cos_kv, sin_kv, cos_q, sin_q = (jnp.tile(a, (B, 1)) for a in (cos_kv, sin_kv, cos_q, sin_q))
    kv = _nsa_kv(x3b, p["nsa_w_kv"], cos_kv, sin_kv, tq, past is None)
    cmp_rows = kv[0]
    from_t = lambda a: jnp.transpose(a.reshape(B, 2, NSA_GROUPS, NSA_HD, a.shape[-1]), (0, 4, 1, 2, 3))
    to_t = lambda a: jnp.transpose(a, (0, 2, 3, 4, 1)).reshape(a.shape[0], 2, NSA_GROUPS * NSA_HD, a.shape[1])
    shp = (B, T, 2, NSA_GROUPS, NSA_HD)
    if past is None:
        n_ch = T // CMP_STRIDE
        n_cmp = n_ch - 1
        n_sel = T // SEL_BLOCK
        cpos = jnp.arange(n_ch, dtype=jnp.int32) * CMP_STRIDE
        cos_c, sin_c = _rope_tables(cpos, (1, 0))
        ckv = _compress_prompt(cmp_rows.reshape(B * n_ch, CMP_STRIDE * 256), p["cmp"], cos_c, sin_c, B, n_ch)
        ov = jnp.pad(_overlap(n_ch, n_sel).T, ((NSA_HD, LANES - NSA_HD - n_sel), (0, 0))).astype(BF16)
        q, gates = _nsa_q(x3b, p["nsa_w_q"], cos_q, sin_q, tq, BF16, p["nsa_w_gate_grp"])
        o = _nsa_attn_prompt(q, gates, ckv, ov, kv[1], B, T, _tile(T, 128), _tile(T, 512), n_cmp, n_sel)
        cmp_out, sel_out = from_t(kv[2]), from_t(kv[3])
        win_out = from_t(kv[4][..., T - min(WINDOW, T):])
    else:
        cache_cmp, cache_sel, page_table, state_win = past
        n_pool = cache_cmp.shape[0]
        n_pages = page_table.shape[1]
        past_len = n_pages * PAGE_SIZE
        PG = _tile(n_pages, 16)
        n_rows = past_len // CMP_STRIDE
        n_sel = past_len // SEL_BLOCK + -(-T // SEL_BLOCK)
        cpos = jnp.arange(n_rows, dtype=jnp.int32) * CMP_STRIDE
        cos_c, sin_c = _rope_tables(cpos, (1, 1))
        sel_rows, win_rows = kv[1], kv[2]
        ckv = _compress_sample(to_t(cache_cmp), page_table, p["cmp"], cos_c, sin_c, PG)
        nb = -(-n_sel // LANES) * LANES
        ovt = jnp.pad(_overlap(n_rows, n_sel).T, ((0, nb - n_sel), (0, 0))).astype(BF16)
        assert T < CMP_STRIDE and T <= SEL_BLOCK
        q = _nsa_q(x3b, p["nsa_w_q"], cos_q, sin_q, tq, F32)[0]
        gates = _nsa_gate(x3b, p["nsa_w_gate_head"], tm_s)
        state_win_t = to_t(state_win)
        o = _nsa_attn_sample(q, gates, ckv, ovt, to_t(cache_sel), page_table,
                             state_win_t, win_rows, sel_rows, T, PG).astype(BF16)
        cmp_out, sel_out = cmp_rows.reshape(shp), sel_rows.reshape(shp)
        win_out = from_t(jnp.concatenate([state_win_t, to_t(win_rows.reshape(shp))], axis=-1)[..., T:])
    x4f, x4b = _mm_ln(o, p["nsa_w_o"], x3f, p["ln_g"][1, 0], p["ln_b"][1, 0], tm_s)
    tb = _tile(M, 1024)
    nb, LT, n_tiles, n_groups = _moe_sizes(M, tb)
    info, info_t, cnt = _router(x4f, p["moe_w_router"], tb)
    plan = _moe_plan(cnt[:, 0, :N_EXPERTS].astype(jnp.int32), LT, n_tiles, n_groups)
    xs = _moe_dispatch(x4b, info_t, plan, tb)
    ys = _moe_ffn(xs, plan, p["moe_w_gate"], p["moe_w_up"], p["moe_w_down"], n_groups, 1792)
    x5f, x5b = _moe_combine(ys, info, x4f, plan, p["ln_g"][1, 1], p["ln_b"][1, 1], tb)
    x6f, _ = _ple(x5b, x5f, pleb[1], p["ple_w_gate"][1], p["ple_w_proj"][1], tm_s)

    return (x6f.reshape(B, T, d), s_new[None], cmp_out, sel_out, win_out)


def kernel(x_prompt, x_sample, state_gla, cache_cmp, cache_sel, state_win, page_table, p_prompt, p_sample,
           ln_g, ln_b, gla_w_in, gla_w_a2, gla_b_a, gla_norm_g, gla_w_o,
           nsa_w_kv, cmp_pe_k, cmp_w1_k, cmp_w2_k, cmp_pe_v, cmp_w1_v, cmp_w2_v,
           nsa_w_in, nsa_w_o, ffn_w_gate, ffn_w_up, ffn_w_down,
           moe_w_router, moe_w_gate, moe_w_up, moe_w_down, ple_w_proj, ple_w_gate):
    w = dict(ln_g=ln_g, ln_b=ln_b, gla_w_in=gla_w_in, gla_w_a2=gla_w_a2, gla_b_a=gla_b_a,
             gla_norm_g=gla_norm_g, gla_w_o=gla_w_o, nsa_w_kv=nsa_w_kv,
             cmp_pe_k=cmp_pe_k, cmp_w1_k=cmp_w1_k, cmp_w2_k=cmp_w2_k,
             cmp_pe_v=cmp_pe_v, cmp_w1_v=cmp_w1_v, cmp_w2_v=cmp_w2_v,
             nsa_w_in=nsa_w_in, nsa_w_o=nsa_w_o, ffn_w_gate=ffn_w_gate, ffn_w_up=ffn_w_up,
             ffn_w_down=ffn_w_down, moe_w_router=moe_w_router, moe_w_gate=moe_w_gate,
             moe_w_up=moe_w_up, moe_w_down=moe_w_down, ple_w_proj=ple_w_proj, ple_w_gate=ple_w_gate)
    p = _prep_weights(w)
    Bp = x_prompt.shape[0]
    gla_zero = jnp.zeros((Bp, GLA_HEADS, GLA_DK, GLA_DV), F32)
    yp, gp, cp, sp, wp = _trunk(x_prompt, p_prompt, 0, gla_zero, None, p)
    past_len = page_table.shape[1] * PAGE_SIZE
    ys, gs, cs, ss, ws = _trunk(x_sample, p_sample, past_len, state_gla[0],
                                (cache_cmp, cache_sel, page_table, state_win), p)
    return (yp, ys, gp, gs, cp, cs, sp, ss, wp, ws)
```
